```python
import functools
import jax
import jax.numpy as jnp
from jax import lax
import numpy as np

D_MODEL = 4096
BATCH = 4
SEQ = 2048
DEPTH = 2
DEC_BATCH = 8
DEC_SEQ = 1
PAST_LEN = 16384
PAGE_SIZE = 128

HEAD_DIM = 128
N_HEADS = D_MODEL // HEAD_DIM
H_A = N_HEADS // 4
H_B = N_HEADS // 4
H_C = N_HEADS - H_A - H_B
D_A = H_A * HEAD_DIM
D_B = H_B * HEAD_DIM
D_C = H_C * HEAD_DIM
D_IN = 2 * D_A + 3 * D_B + 3 * D_C + H_C
CHUNK = 128
Q_BLOCK = 128
CONV_W = 3
D_FF = ((8 * D_MODEL // 3 + 127) // 128) * 128
FORGET_BIAS = 5.0
ALPHA = (2 * DEPTH) ** 0.25
BETA = (8 * DEPTH) ** -0.25
EPS = 1e-5

kernel_name = 'parallel_hybrid_fox_sgu_shortconv_decode'


def layer_norm(x, g, b):
    xf = x.astype(jnp.float32)
    mu = jnp.mean(xf, axis=-1, keepdims=True)
    var = jnp.mean(jnp.square(xf - mu), axis=-1, keepdims=True)
    y = (xf - mu) * lax.rsqrt(var + EPS) * g.astype(jnp.float32) + b.astype(jnp.float32)
    return y.astype(x.dtype)


def head_rms_norm(y, g):
    b, t, _ = y.shape
    yh = y.astype(jnp.float32).reshape(b, t, N_HEADS, HEAD_DIM)
    yh = yh * lax.rsqrt(jnp.mean(jnp.square(yh), axis=-1, keepdims=True) + EPS)
    return (yh.reshape(b, t, D_MODEL) * g.astype(jnp.float32)).astype(y.dtype)


def causal_conv(x_ext, w, b):
    t = x_ext.shape[1] - (CONV_W - 1)
    y = b.astype(x_ext.dtype)
    for j in range(CONV_W):
        y = y + x_ext[:, j:j + t] * w[j]
    return y


def chunk_spatial_gate(u, v, w_s, b_s):
    b, t, _ = v.shape
    n_chunks = -(-t // CHUNK)
    vp = jnp.pad(v, ((0, 0), (0, n_chunks * CHUNK - t), (0, 0))).reshape(b, n_chunks, CHUNK, H_A, HEAD_DIM)
    causal = jnp.tril(jnp.ones((CHUNK, CHUNK), dtype=bool))
    w = jnp.where(causal[None], w_s, jnp.zeros_like(w_s)).astype(v.dtype)
    mixed = jnp.einsum('hts,bcshd->bcthd', w, vp) + b_s.T.astype(v.dtype)[None, None, :, :, None]
    mixed = mixed.reshape(b, n_chunks * CHUNK, D_A)[:, :t]
    return u * mixed


def fox_attention_block(q, k, v, fq, fk, q_pos, k_pos):
    s = jnp.einsum('bqhd,bkhd->bhqk', q, k).astype(jnp.float32) * (HEAD_DIM ** -0.5)
    s = s + jnp.transpose(fq, (0, 2, 1))[:, :, :, None] - jnp.transpose(fk, (0, 2, 1))[:, :, None, :]
    s = jnp.where(k_pos[None, :] <= q_pos[:, None], s, -jnp.inf)
    p = jax.nn.softmax(s, axis=-1).astype(v.dtype)
    return jnp.einsum('bhqk,bkhd->bqhd', p, v)


def fox_attend_prompt(q, k, v, logf):
    f_cum = lax.cumsum(logf, axis=1)
    t = q.shape[1]
    outs = []
    for start in range(0, t, Q_BLOCK):
        end = min(start + Q_BLOCK, t)
        outs.append(fox_attention_block(q[:, start:end], k[:, :end], v[:, :end],
                                        f_cum[:, start:end], f_cum[:, :end],
                                        jnp.arange(start, end), jnp.arange(end)))
    return jnp.concatenate(outs, axis=1)


def fox_attend_sample(q, k, v, logf, k_past, v_past, lf_past):
    p_len = k_past.shape[1]
    t = q.shape[1]
    lf_past = lf_past.astype(jnp.float32)
    r_past = lax.cumsum(lf_past, axis=1, reverse=True) - lf_past
    g_new = lax.cumsum(logf, axis=1)
    gq = jnp.transpose(g_new, (0, 2, 1))[:, :, :, None]
    scale = HEAD_DIM ** -0.5
    s_past = jnp.einsum('bqhd,bkhd->bhqk', q, k_past.astype(q.dtype)).astype(jnp.float32) * scale
    s_past = s_past + gq + jnp.transpose(r_past, (0, 2, 1))[:, :, None, :]
    s_new = jnp.einsum('bqhd,bkhd->bhqk', q, k).astype(jnp.float32) * scale
    s_new = s_new + gq - jnp.transpose(g_new, (0, 2, 1))[:, :, None, :]
    causal = jnp.arange(t)[None, :] <= jnp.arange(t)[:, None]
    s_new = jnp.where(causal, s_new, -jnp.inf)
    p = jax.nn.softmax(jnp.concatenate([s_past, s_new], axis=-1), axis=-1).astype(v.dtype)
    return (jnp.einsum('bhqk,bkhd->bqhd', p[..., :p_len], v_past.astype(v.dtype))
            + jnp.einsum('bhqk,bkhd->bqhd', p[..., p_len:], v))


def token_mixers(x, conv_buf, attend, w_in, b_f, sgu_ln_g, sgu_ln_b, sgu_w, sgu_b, conv_w, conv_b):
    bsz, t, _ = x.shape
    z = jnp.einsum('btd,de->bte', x, w_in)
    sizes = (D_A, D_A, D_B, D_B, D_B, D_C, D_C, D_C)
    u_a, v_a, x_b, g_b, g_c, q, k, v, f = jnp.split(z, [int(i) for i in np.cumsum(sizes)], axis=-1)
    u_a = jax.nn.gelu(u_a)
    v_a = layer_norm(jax.nn.gelu(v_a), sgu_ln_g, sgu_ln_b)
    y_a = chunk_spatial_gate(u_a, v_a, sgu_w, sgu_b)
    c_in = g_c * x_b
    c_ext = jnp.concatenate([conv_buf.astype(c_in.dtype), c_in], axis=1)
    y_b = g_b * causal_conv(c_ext, conv_w, conv_b)
    qh = q.reshape(bsz, t, H_C, HEAD_DIM)
    kh = k.reshape(bsz, t, H_C, HEAD_DIM)
    vh = v.reshape(bsz, t, H_C, HEAD_DIM)
    logf = jax.nn.log_sigmoid((f + b_f).astype(jnp.float32))
    y_c = attend(qh, kh, vh, logf).reshape(bsz, t, D_C)
    last_chunk = ((t - 1) // CHUNK) * CHUNK
    mix = jnp.concatenate([y_a, y_b, y_c.astype(y_a.dtype)], axis=-1)
    return mix, (kh, vh, logf, c_ext[:, -(CONV_W - 1):], v_a[:, last_chunk:])


def decoder_layer(x, conv_buf, ffn_buf, attend, w_in, b_f, sgu_ln_g, sgu_ln_b, sgu_w, sgu_b,
                  conv_w, conv_b, mix_norm_g, w_o, ln1_g, ln1_b, w_up, ffn_conv_w, ffn_conv_b,
                  w_down, ln2_g, ln2_b):
    mix, states = token_mixers(x, conv_buf, attend, w_in, b_f, sgu_ln_g, sgu_ln_b, sgu_w, sgu_b,
                               conv_w, conv_b)
    h = jnp.einsum('btd,de->bte', head_rms_norm(mix, mix_norm_g), w_o)
    x = layer_norm(ALPHA * x + h, ln1_g, ln1_b)
    up = jnp.einsum('btd,df->btf', x, w_up)
    up_ext = jnp.concatenate([ffn_buf.astype(up.dtype), up], axis=1)
    gate, val = jnp.split(causal_conv(up_ext, ffn_conv_w, ffn_conv_b), 2, axis=-1)
    ff = jnp.einsum('btf,fd->btd', jax.nn.silu(gate) * val, w_down)
    x = layer_norm(ALPHA * x + ff, ln2_g, ln2_b)
    return x, states + (up_ext[:, -(CONV_W - 1):],)


def setup_inputs(seed: int = 0) -> dict:
    key = jax.random.key(seed)
    ks = jax.random.split(key, 32)
    f32 = jnp.float32

    def nrm(k, shape, scale):
        return jax.random.normal(k, shape, f32) * scale

    n_pages = PAST_LEN // PAGE_SIZE
    n_used = DEC_BATCH * n_pages
    n_pool = n_used + max(1, n_used // 4)
    return {
        'x_prompt': nrm(ks[0], (BATCH, SEQ, D_MODEL), 1.0),
        'x_sample': nrm(ks[1], (DEC_BATCH, DEC_SEQ, D_MODEL), 1.0),
        'cache_k': nrm(ks[2], (DEPTH, n_pool, PAGE_SIZE, H_C, HEAD_DIM), 1.0),
        'cache_v': nrm(ks[3], (DEPTH, n_pool, PAGE_SIZE, H_C, HEAD_DIM), 1.0),
        'cache_logf': jax.nn.log_sigmoid(FORGET_BIAS + nrm(ks[4], (DEPTH, n_pool, PAGE_SIZE, H_C), 1.0)),
        'page_table': jax.random.permutation(ks[5], n_pool)[:n_used].reshape(DEC_BATCH, n_pages).astype(jnp.int32),
        'state_conv': nrm(ks[6], (DEPTH, DEC_BATCH, CONV_W - 1, D_B), 1.0),
        'state_ffn_conv': nrm(ks[7], (DEPTH, DEC_BATCH, CONV_W - 1, 2 * D_FF), 1.0),
        'w_in': nrm(ks[8], (DEPTH, D_MODEL, D_IN), D_MODEL ** -0.5),
        'b_f': FORGET_BIAS + nrm(ks[9], (DEPTH, H_C), 0.1),
        'sgu_ln_g': 1.0 + nrm(ks[10], (DEPTH, D_A), 0.01),
        'sgu_ln_b': nrm(ks[11], (DEPTH, D_A), 0.01),
        'sgu_w': nrm(ks[12], (DEPTH, H_A, CHUNK, CHUNK), CHUNK ** -0.5),
        'sgu_b': 1.0 + nrm(ks[13], (DEPTH, H_A, CHUNK), 0.01),
        'conv_w': nrm(ks[14], (DEPTH, CONV_W, D_B), CONV_W ** -0.5),
        'conv_b': nrm(ks[15], (DEPTH, D_B), 0.01),
        'mix_norm_g': 1.0 + nrm(ks[16], (DEPTH, D_MODEL), 0.01),
        'w_o': nrm(ks[17], (DEPTH, D_MODEL, D_MODEL), BETA * D_MODEL ** -0.5),
        'ln1_g': 1.0 + nrm(ks[18], (DEPTH, D_MODEL), 0.01),
        'ln1_b': nrm(ks[19], (DEPTH, D_MODEL), 0.01),
        'w_up': nrm(ks[20], (DEPTH, D_MODEL, 2 * D_FF), D_MODEL ** -0.5),
        'ffn_conv_w': nrm(ks[21], (DEPTH, CONV_W, 2 * D_FF), CONV_W ** -0.5),
        'ffn_conv_b': nrm(ks[22], (DEPTH, 2 * D_FF), 0.01),
        'w_down': nrm(ks[23], (DEPTH, D_FF, D_MODEL), BETA * D_FF ** -0.5),
        'ln2_g': 1.0 + nrm(ks[24], (DEPTH, D_MODEL), 0.01),
        'ln2_b': nrm(ks[25], (DEPTH, D_MODEL), 0.01),
    }


def reference(x_prompt, x_sample, cache_k, cache_v, cache_logf, page_table, state_conv, state_ffn_conv,
              w_in, b_f, sgu_ln_g, sgu_ln_b, sgu_w, sgu_b, conv_w, conv_b, mix_norm_g, w_o,
              ln1_g, ln1_b, w_up, ffn_conv_w, ffn_conv_b, w_down, ln2_g, ln2_b):
    n_seq, n_pages = page_table.shape
    past = n_pages * PAGE_SIZE
    xp, xs = x_prompt, x_sample
    sp, ss = [], []
    for l in range(DEPTH):
        lw = (w_in[l], b_f[l], sgu_ln_g[l], sgu_ln_b[l], sgu_w[l], sgu_b[l], conv_w[l], conv_b[l],
              mix_norm_g[l], w_o[l], ln1_g[l], ln1_b[l], w_up[l], ffn_conv_w[l], ffn_conv_b[l],
              w_down[l], ln2_g[l], ln2_b[l])
        zc = jnp.zeros((xp.shape[0], CONV_W - 1, D_B), xp.dtype)
        zf = jnp.zeros((xp.shape[0], CONV_W - 1, 2 * D_FF), xp.dtype)
        xp, st = decoder_layer(xp, zc, zf, fox_attend_prompt, *lw)
        sp.append(st)
        attend_s = functools.partial(
            fox_attend_sample,
            k_past=cache_k[l, page_table].reshape(n_seq, past, H_C, HEAD_DIM),
            v_past=cache_v[l, page_table].reshape(n_seq, past, H_C, HEAD_DIM),
            lf_past=cache_logf[l, page_table].reshape(n_seq, past, H_C))
        xs, st = decoder_layer(xs, state_conv[l], state_ffn_conv[l], attend_s, *lw)
        ss.append(st)

    def stack(sts, i):
        return jnp.stack([s[i] for s in sts])

    k_prompt, v_prompt, logf_prompt = stack(sp, 0), stack(sp, 1), stack(sp, 2)
    conv_prompt, sgu_v_prompt, ffn_conv_prompt = stack(sp, 3), stack(sp, 4), stack(sp, 5)
    k_sample, v_sample, logf_sample = stack(ss, 0), stack(ss, 1), stack(ss, 2)
    conv_sample, sgu_v_sample, ffn_conv_sample = stack(ss, 3), stack(ss, 4), stack(ss, 5)
    return (xp, xs, k_prompt, v_prompt, logf_prompt, conv_prompt, ffn_conv_prompt, sgu_v_prompt,
            k_sample, v_sample, logf_sample, conv_sample, ffn_conv_sample, sgu_v_sample)
```

```python
import functools

import jax
import jax.numpy as jnp
from jax import lax
from jax.experimental import pallas as pl
from jax.experimental.pallas import tpu as pltpu

HEAD_DIM = 128
CHUNK = 128
CONV_W = 3
EPS = 1e-5
MASK_VALUE = -1e30
SUBLANES = 8
VMEM_LIMIT_BYTES = 56 * 1024 * 1024
PAGES_PER_STEP = 4

F32 = jnp.float32
BF16 = jnp.bfloat16


def _pick(dim, candidates):
    for c in candidates:
        if dim % c == 0:
            return c
    return dim


def _params(*semantics):
    return pltpu.CompilerParams(dimension_semantics=semantics, vmem_limit_bytes=VMEM_LIMIT_BYTES)


def _mm_kernel(x_ref, w_ref, o_ref, acc_ref, *, nk):
    if nk == 1:
        o_ref[...] = jnp.dot(x_ref[...], w_ref[...], preferred_element_type=F32).astype(o_ref.dtype)
        return
    k = pl.program_id(2)

    @pl.when(k == 0)
    def _():
        acc_ref[...] = jnp.dot(x_ref[...], w_ref[...], preferred_element_type=F32)

    @pl.when(k > 0)
    def _():
        acc_ref[...] += jnp.dot(x_ref[...], w_ref[...], preferred_element_type=F32)

    @pl.when(k == nk - 1)
    def _():
        o_ref[...] = acc_ref[...].astype(o_ref.dtype)


def _matmul(x, w, *, col_off=0, n_cols=None, out_dtype=F32, name="mm"):
    m, kdim = x.shape
    n_cols = w.shape[1] if n_cols is None else n_cols
    tm = _pick(m, (1024, 512, 256, 128))
    tk = _pick(kdim, (5504, 4096, 2048, 1024, 512) if m <= 64 else (1024, 5504, 512, 256, 128))
    tn_cands = (512, 256, 128) if tk > 4096 else (1024, 512, 256, 128)
    tn = next(c for c in tn_cands if n_cols % c == 0 and col_off % c == 0)
    joff = col_off // tn
    nk = kdim // tk
    grid = (m // tm, n_cols // tn, nk)
    return pl.pallas_call(
        functools.partial(_mm_kernel, nk=nk),
        grid=grid,
        in_specs=[pl.BlockSpec((tm, tk), lambda i, j, k: (i, k)),
                  pl.BlockSpec((tk, tn), lambda i, j, k: (k, j + joff))],
        out_specs=pl.BlockSpec((tm, tn), lambda i, j, k: (i, j)),
        out_shape=jax.ShapeDtypeStruct((m, n_cols), out_dtype),
        scratch_shapes=[pltpu.VMEM((tm, tn), F32)],
        compiler_params=_params("parallel", "parallel", "arbitrary"),
        name=name,
    )(x, w)


def _logf_kernel(x_ref, w_ref, b_ref, o_ref):
    z = jnp.dot(x_ref[...], w_ref[...], preferred_element_type=F32) + b_ref[...]
    o_ref[...] = jnp.minimum(z, 0.0) - jnp.log1p(jnp.exp(-jnp.abs(z)))


def _logf(x, w_f, b_f):
    m, kdim = x.shape
    tm = _pick(m, (512, 256, 128))
    return pl.pallas_call(
        _logf_kernel,
        grid=(m // tm,),
        in_specs=[pl.BlockSpec((tm, kdim), lambda i: (i, 0)),
                  pl.BlockSpec((kdim, HEAD_DIM), lambda i: (0, 0)),
                  pl.BlockSpec((1, HEAD_DIM), lambda i: (0, 0))],
        out_specs=pl.BlockSpec((tm, HEAD_DIM), lambda i: (i, 0)),
        out_shape=jax.ShapeDtypeStruct((m, HEAD_DIM), F32),
        compiler_params=_params("parallel"),
        name="logf",
    )(x, w_f, b_f)


def _head_rms(y, g):
    return y * lax.rsqrt(jnp.mean(y * y, axis=-1, keepdims=True) + EPS) * g


def _shift_rows(c, prev):
    row = lax.broadcasted_iota(jnp.int32, c.shape, 0)
    p1 = prev[SUBLANES - 1:SUBLANES, :]
    p2 = prev[SUBLANES - 2:SUBLANES - 1, :]
    c1 = jnp.where(row == 0, p1, pltpu.roll(c, 1, axis=0))
    c2 = jnp.where(row == 0, p2, jnp.where(row == 1, p1, pltpu.roll(c, 2, axis=0)))
    return c1, c2


def _sgu_kernel(u_ref, v_ref, lng_ref, lnb_ref, w_ref, bst_ref, g_ref, y_ref, vln_ref, *, n_chunks, n_heads):
    u = jax.nn.gelu(u_ref[...].astype(F32))
    v = jax.nn.gelu(v_ref[...].astype(F32))
    mu = jnp.mean(v, axis=-1, keepdims=True)
    d = v - mu
    var = jnp.mean(d * d, axis=-1, keepdims=True)
    vln = d * lax.rsqrt(var + EPS) * lng_ref[...] + lnb_ref[...]
    vln_ref[...] = vln
    vb = vln.astype(BF16)
    t_idx = lax.broadcasted_iota(jnp.int32, (CHUNK, CHUNK), 0)
    s_idx = lax.broadcasted_iota(jnp.int32, (CHUNK, CHUNK), 1)
    causal = s_idx <= t_idx
    for h in range(n_heads):
        cols = slice(h * HEAD_DIM, (h + 1) * HEAD_DIM)
        wm = jnp.where(causal, w_ref[h], 0.0).astype(BF16)
        bias = bst_ref[:, h:h + 1]
        g = g_ref[:, cols]
        for c in range(n_chunks):
            rows = slice(c * CHUNK, (c + 1) * CHUNK)
            mixed = jnp.dot(wm, vb[rows, cols], preferred_element_type=F32) + bias
            y_ref[rows, cols] = _head_rms(u[rows, cols] * mixed, g).astype(y_ref.dtype)


def _sgu(zb, ln_g, ln_b, w_s, b_s_t, norm_g, d_a):
    m = zb.shape[0]
    n_heads = d_a // HEAD_DIM
    tm = _pick(m, (256, 128))
    return pl.pallas_call(
        functools.partial(_sgu_kernel, n_chunks=tm // CHUNK, n_heads=n_heads),
        grid=(m // tm,),
        in_specs=[pl.BlockSpec((tm, d_a), lambda i: (i, 0)),
                  pl.BlockSpec((tm, d_a), lambda i: (i, 1)),
                  pl.BlockSpec((1, d_a), lambda i: (0, 0)),
                  pl.BlockSpec((1, d_a), lambda i: (0, 0)),
                  pl.BlockSpec((n_heads, CHUNK, CHUNK), lambda i: (0, 0, 0)),
                  pl.BlockSpec((CHUNK, n_heads), lambda i: (0, 0)),
                  pl.BlockSpec((1, d_a), lambda i: (0, 0))],
        out_specs=[pl.BlockSpec((tm, d_a), lambda i: (i, 0)),
                   pl.BlockSpec((tm, d_a), lambda i: (i, 0))],
        out_shape=[jax.ShapeDtypeStruct((m, d_a), BF16), jax.ShapeDtypeStruct((m, d_a), F32)],
        compiler_params=_params("parallel"),
        name="sgu",
    )(zb, zb, ln_g, ln_b, w_s, b_s_t, norm_g)


def _bconv_kernel(*refs, tiles_per_seq, n_heads, decode):
    if decode:
        x_ref, gb_ref, gc_ref, cw_ref, cb_ref, g_ref, p2_ref, p1_ref, y_ref, st_ref = refs
    else:
        x_ref, gb_ref, gc_ref, cw_ref, cb_ref, g_ref, y_ref, st_ref, carry_ref = refs
    c = gc_ref[...].astype(F32) * x_ref[...].astype(F32)
    tm = c.shape[0]
    if decode:
        c1, c2 = p1_ref[...], p2_ref[...]
    else:
        start = pl.program_id(0) % tiles_per_seq == 0
        prev = jnp.where(start, 0.0, carry_ref[...])
        c1, c2 = _shift_rows(c, prev)
        carry_ref[...] = c[tm - SUBLANES:, :]
    st_ref[...] = c[tm - SUBLANES:, :]
    conv = cb_ref[...] + cw_ref[0:1, :] * c2 + cw_ref[1:2, :] * c1 + cw_ref[2:3, :] * c
    y = gb_ref[...].astype(F32) * conv
    for h in range(n_heads):
        cols = slice(h * HEAD_DIM, (h + 1) * HEAD_DIM)
        y_ref[:, cols] = _head_rms(y[:, cols], g_ref[:, cols]).astype(y_ref.dtype)


def _bconv(zb, conv_w, conv_b, norm_g, d_b, col_blk, seq_len, state=None):
    m = zb.shape[0]
    decode = state is not None
    tm = m if decode else _pick(seq_len, (512, 256, 128))
    n_tiles = m // tm
    row = lambda i: (i, 0)
    fixed = lambda i: (0, 0)
    in_specs = [pl.BlockSpec((tm, d_b), lambda i: (i, col_blk)),
                pl.BlockSpec((tm, d_b), lambda i: (i, col_blk + 1)),
                pl.BlockSpec((tm, d_b), lambda i: (i, col_blk + 2)),
                pl.BlockSpec((CONV_W, d_b), fixed),
                pl.BlockSpec((1, d_b), fixed),
                pl.BlockSpec((1, d_b), fixed)]
    args = [zb, zb, zb, conv_w, conv_b, norm_g]
    scratch = []
    if decode:
        in_specs += [pl.BlockSpec((tm, d_b), row), pl.BlockSpec((tm, d_b), row)]
        args += [state[:, 0, :], state[:, 1, :]]
    else:
        scratch = [pltpu.VMEM((SUBLANES, d_b), F32)]
    return pl.pallas_call(
        functools.partial(_bconv_kernel, tiles_per_seq=max(seq_len // tm, 1), n_heads=d_b // HEAD_DIM,
                          decode=decode),
        grid=(n_tiles,),
        in_specs=in_specs,
        out_specs=[pl.BlockSpec((tm, d_b), row),
                   pl.BlockSpec((None, SUBLANES, d_b), lambda i: (i, 0, 0))],
        out_shape=[jax.ShapeDtypeStruct((m, d_b), BF16),
                   jax.ShapeDtypeStruct((n_tiles, SUBLANES, d_b), F32)],
        scratch_shapes=scratch,
        compiler_params=_params("arbitrary"),
        name="bconv_decode" if decode else "bconv",
    )(*args)


def _cumsum_kernel(x_ref, o_ref):
    x = x_ref[...]
    t = x.shape[-1]
    lane = lax.broadcasted_iota(jnp.int32, x.shape, 1)
    shift = 1
    while shift < t:
        x = x + jnp.where(lane >= shift, pltpu.roll(x, shift, axis=1), 0.0)
        shift *= 2
    o_ref[...] = x


def _cumsum_time(logf_t):
    b, h, t = logf_t.shape
    return pl.pallas_call(
        _cumsum_kernel,
        grid=(b,),
        in_specs=[pl.BlockSpec((None, h, t), lambda i: (i, 0, 0))],
        out_specs=pl.BlockSpec((None, h, t), lambda i: (i, 0, 0)),
        out_shape=jax.ShapeDtypeStruct((b, h, t), F32),
        compiler_params=_params("parallel"),
        name="cumsum_logf",
    )(logf_t)


def _fox_prompt_kernel(q_ref, k_ref, v_ref, fq_ref, fk_ref, g_ref, o_ref, m_ref, l_ref, acc_ref, *, tq, tk):
    qi = pl.program_id(2)
    kj = pl.program_id(3)

    @pl.when(kj == 0)
    def _():
        m_ref[...] = jnp.full(m_ref.shape, MASK_VALUE, F32)
        l_ref[...] = jnp.zeros(l_ref.shape, F32)
        acc_ref[...] = jnp.zeros(acc_ref.shape, F32)

    @pl.when(kj <= qi)
    def _():
        s = lax.dot_general(q_ref[...], k_ref[...].astype(BF16), (((1,), (1,)), ((), ())),
                            preferred_element_type=F32)
        s = s * (HEAD_DIM ** -0.5) + fq_ref[...] - fk_ref[...]
        q_pos = qi * tq + lax.broadcasted_iota(jnp.int32, s.shape, 0)
        k_pos = kj * tk + lax.broadcasted_iota(jnp.int32, s.shape, 1)
        s = jnp.where(k_pos <= q_pos, s, MASK_VALUE)
        m_prev = m_ref[...]
        m_new = jnp.maximum(m_prev, jnp.max(s, axis=-1, keepdims=True))
        p = jnp.exp(s - m_new)
        alpha = jnp.exp(m_prev - m_new)
        l_ref[...] = alpha * l_ref[...] + jnp.sum(p, axis=-1, keepdims=True)
        acc_ref[...] = alpha * acc_ref[...] + jnp.dot(p.astype(BF16), v_ref[...].astype(BF16),
                                                      preferred_element_type=F32)
        m_ref[...] = m_new

    @pl.when(kj == qi)
    def _():
        o = acc_ref[...] / l_ref[...]
        o_ref[...] = _head_rms(o, g_ref[...]).astype(o_ref.dtype)


def _fox_prompt(zb, k, v, f_col, f_row, norm_g, *, batch, seq_len, n_heads, q_col_blk):
    m = zb.shape[0]
    tq = tk = _pick(seq_len, (512, 256, 128))
    nq = seq_len // tq
    q_map = lambda b, h, qi, kj: (b * nq + qi, q_col_blk + h)
    kv_map = lambda b, h, qi, kj: (b * nq + jnp.minimum(kj, qi), h)
    return pl.pallas_call(
        functools.partial(_fox_prompt_kernel, tq=tq, tk=tk),
        grid=(batch, n_heads, nq, nq),
        in_specs=[pl.BlockSpec((tq, HEAD_DIM), q_map),
                  pl.BlockSpec((tk, HEAD_DIM), kv_map),
                  pl.BlockSpec((tk, HEAD_DIM), kv_map),
                  pl.BlockSpec((None, None, tq, 1), lambda b, h, qi, kj: (b, h, qi, 0)),
                  pl.BlockSpec((None, None, 1, tk), lambda b, h, qi, kj: (b, h, 0, jnp.minimum(kj, qi))),
                  pl.BlockSpec((1, HEAD_DIM), lambda b, h, qi, kj: (0, h))],
        out_specs=pl.BlockSpec((tq, HEAD_DIM), lambda b, h, qi, kj: (b * nq + qi, h)),
        out_shape=jax.ShapeDtypeStruct((m, n_heads * HEAD_DIM), BF16),
        scratch_shapes=[pltpu.VMEM((tq, 1), F32), pltpu.VMEM((tq, 1), F32), pltpu.VMEM((tq, HEAD_DIM), F32)],
        compiler_params=_params("parallel", "parallel", "arbitrary", "arbitrary"),
        name="fox_prompt",
    )(zb, k, v, f_col, f_row, norm_g)


def _fox_decode_kernel(pt_ref, q_ref, kn_ref, vn_ref, lfn_ref, g_ref, *rest, n_heads, page):
    del pt_ref
    pps = PAGES_PER_STEP
    k_refs, v_refs, lf_refs = rest[:pps], rest[pps:2 * pps], rest[2 * pps:3 * pps]
    o_ref, m_ref, l_ref, acc_ref, carry_ref = rest[3 * pps:]
    step = pl.program_id(1)
    scale = HEAD_DIM ** -0.5
    span = pps * page
    head_row = lax.broadcasted_iota(jnp.int32, (n_heads, 1), 0)

    @pl.when(step == 0)
    def _():
        m_ref[...] = jnp.full(m_ref.shape, MASK_VALUE, F32)
        l_ref[...] = jnp.zeros(l_ref.shape, F32)
        acc_ref[...] = jnp.zeros(acc_ref.shape, F32)
        carry_ref[...] = jnp.zeros(carry_ref.shape, F32)

    q = q_ref[...]
    lf = jnp.concatenate([r[...] for r in lf_refs], axis=1)
    lane = lax.broadcasted_iota(jnp.int32, lf.shape, 1)
    suffix = lf
    shift = 1
    while shift < span:
        suffix = suffix + jnp.where(lane < span - shift, pltpu.roll(suffix, span - shift, axis=1), 0.0)
        shift *= 2
    carry = carry_ref[...]
    bias = suffix - lf + carry + lfn_ref[...]
    carry_ref[...] = carry + suffix[:, 0:1]

    s = jnp.zeros((n_heads, span), F32)
    for h in range(n_heads):
        kh = jnp.concatenate([r[:, h, :] for r in k_refs], axis=0).astype(BF16)
        sh = lax.dot_general(q, kh, (((1,), (1,)), ((), ())), preferred_element_type=F32)
        s = jnp.where(head_row == h, sh, s)
    s = s * scale + bias
    m_prev = m_ref[...]
    m_new = jnp.maximum(m_prev, jnp.max(s, axis=-1, keepdims=True))
    p = jnp.exp(s - m_new)
    alpha = jnp.exp(m_prev - m_new)
    l_ref[...] = alpha * l_ref[...] + jnp.sum(p, axis=-1, keepdims=True)
    pb = p.astype(BF16)
    pv = jnp.zeros((n_heads, HEAD_DIM), F32)
    for h in range(n_heads):
        vh = jnp.concatenate([r[:, h, :] for r in v_refs], axis=0).astype(BF16)
        pv = jnp.where(head_row == h, jnp.dot(pb, vh, preferred_element_type=F32), pv)
    acc_ref[...] = alpha * acc_ref[...] + pv
    m_ref[...] = m_new

    @pl.when(step == pl.num_programs(1) - 1)
    def _():
        kn = kn_ref[...].astype(BF16).astype(F32)
        vn = vn_ref[...].astype(BF16).astype(F32)
        s_self = jnp.sum(q.astype(F32) * kn, axis=-1, keepdims=True) * scale
        m_prev = m_ref[...]
        m_fin = jnp.maximum(m_prev, s_self)
        a = jnp.exp(m_prev - m_fin)
        p_self = jnp.exp(s_self - m_fin)
        l_fin = a * l_ref[...] + p_self
        o = (a * acc_ref[...] + p_self.astype(BF16).astype(F32) * vn) / l_fin
        o_ref[...] = _head_rms(o, g_ref[...]).astype(o_ref.dtype)


def _fox_decode(q, k_new, v_new, lf_new, norm_g, cache_k, cache_v, cache_lf_t, page_table, layer):
    bsz, n_heads, _ = q.shape
    page = cache_k.shape[2]
    n_pages = page_table.shape[1]
    pps = PAGES_PER_STEP
    assert n_pages % pps == 0
    n_steps = n_pages // pps

    def page_idx(b, s, pt, r):
        return pt[b, (n_steps - 1 - s) * pps + r]

    per_b = lambda b, s, pt: (b, 0, 0)
    kv_spec = lambda r: pl.BlockSpec((None, None, page, n_heads, HEAD_DIM),
                                     lambda b, s, pt: (layer, page_idx(b, s, pt, r), 0, 0, 0))
    lf_spec = lambda r: pl.BlockSpec((None, None, n_heads, page),
                                     lambda b, s, pt: (layer, page_idx(b, s, pt, r), 0, 0))
    in_specs = ([pl.BlockSpec((None, n_heads, HEAD_DIM), per_b)] * 3
                + [pl.BlockSpec((None, n_heads, 1), per_b),
                   pl.BlockSpec((n_heads, HEAD_DIM), lambda b, s, pt: (0, 0))]
                + [kv_spec(r) for r in range(pps)] * 2
                + [lf_spec(r) for r in range(pps)])
    grid_spec = pltpu.PrefetchScalarGridSpec(
        num_scalar_prefetch=1,
        grid=(bsz, n_steps),
        in_specs=in_specs,
        out_specs=pl.BlockSpec((None, n_heads, HEAD_DIM), per_b),
        scratch_shapes=[pltpu.VMEM((n_heads, 1), F32), pltpu.VMEM((n_heads, 1), F32),
                        pltpu.VMEM((n_heads, HEAD_DIM), F32), pltpu.VMEM((n_heads, 1), F32)])
    return pl.pallas_call(
        functools.partial(_fox_decode_kernel, n_heads=n_heads, page=page),
        grid_spec=grid_spec,
        out_shape=jax.ShapeDtypeStruct((bsz, n_heads, HEAD_DIM), BF16),
        compiler_params=_params("parallel", "arbitrary"),
        name="fox_decode",
    )(page_table, q, k_new, v_new, lf_new, norm_g,
      *([cache_k] * pps), *([cache_v] * pps), *([cache_lf_t] * pps))


def _oproj_kernel(a_ref, b_ref, c_ref, wa_ref, wb_ref, wc_ref, o_ref):
    acc = jnp.dot(a_ref[...], wa_ref[...], preferred_element_type=F32)
    acc += jnp.dot(b_ref[...], wb_ref[...], preferred_element_type=F32)
    acc += jnp.dot(c_ref[...], wc_ref[...], preferred_element_type=F32)
    o_ref[...] = acc


def _oproj(ya, yb, yc, w_o):
    m, d_a = ya.shape
    d_c = yc.shape[1]
    d = w_o.shape[1]
    assert yb.shape[1] == d_a and d_c == 2 * d_a and w_o.shape[0] == 2 * d_a + d_c
    tm = _pick(m, (1024, 512, 256, 128))
    tn = _pick(d, (1024, 512, 256, 128))
    row = lambda i, j: (i, 0)
    return pl.pallas_call(
        _oproj_kernel,
        grid=(m // tm, d // tn),
        in_specs=[pl.BlockSpec((tm, d_a), row), pl.BlockSpec((tm, d_a), row), pl.BlockSpec((tm, d_c), row),
                  pl.BlockSpec((d_a, tn), lambda i, j: (0, j)),
                  pl.BlockSpec((d_a, tn), lambda i, j: (1, j)),
                  pl.BlockSpec((d_c, tn), lambda i, j: (1, j))],
        out_specs=pl.BlockSpec((tm, tn), lambda i, j: (i, j)),
        out_shape=jax.ShapeDtypeStruct((m, d), F32),
        compiler_params=_params("parallel", "parallel"),
        name="oproj",
    )(ya, yb, yc, w_o, w_o, w_o)


def _ln_res_kernel(x_ref, h_ref, g_ref, b_ref, o_ref, ob_ref, *, alpha):
    y = alpha * x_ref[...] + h_ref[...]
    mu = jnp.mean(y, axis=-1, keepdims=True)
    d = y - mu
    var = jnp.mean(d * d, axis=-1, keepdims=True)
    o = d * lax.rsqrt(var + EPS) * g_ref[...] + b_ref[...]
    o_ref[...] = o
    ob_ref[...] = o.astype(ob_ref.dtype)


def _ln_res(x, h, g, b, alpha):
    m, d = x.shape
    tm = _pick(m, (256, 128))
    row = lambda i: (i, 0)
    fixed = lambda i: (0, 0)
    return pl.pallas_call(
        functools.partial(_ln_res_kernel, alpha=alpha),
        grid=(m // tm,),
        in_specs=[pl.BlockSpec((tm, d), row), pl.BlockSpec((tm, d), row),
                  pl.BlockSpec((1, d), fixed), pl.BlockSpec((1, d), fixed)],
        out_specs=[pl.BlockSpec((tm, d), row), pl.BlockSpec((tm, d), row)],
        out_shape=[jax.ShapeDtypeStruct((m, d), F32), jax.ShapeDtypeStruct((m, d), BF16)],
        compiler_params=_params("parallel"),
        name="ln_res",
    )(x, h, g, b)


def _ffn_up_kernel(*refs, tiles_per_seq, decode):
    if decode:
        (x_ref, wg_ref, wv_ref, cwg_ref, cwv_ref, cbg_ref, cbv_ref, pg2_ref, pg1_ref, pv2_ref, pv1_ref,
         h_ref, sg_ref, sv_ref) = refs
    else:
        (x_ref, wg_ref, wv_ref, cwg_ref, cwv_ref, cbg_ref, cbv_ref,
         h_ref, sg_ref, sv_ref, carry_g_ref, carry_v_ref) = refs
    j = pl.program_id(1)
    x = x_ref[...]
    ug = jnp.dot(x, wg_ref[...], preferred_element_type=F32)
    uv = jnp.dot(x, wv_ref[...], preferred_element_type=F32)
    tm = ug.shape[0]
    if decode:
        ug1, ug2, uv1, uv2 = pg1_ref[...], pg2_ref[...], pv1_ref[...], pv2_ref[...]
    else:
        start = pl.program_id(0) % tiles_per_seq == 0
        ug1, ug2 = _shift_rows(ug, jnp.where(start, 0.0, carry_g_ref[j]))
        uv1, uv2 = _shift_rows(uv, jnp.where(start, 0.0, carry_v_ref[j]))
        carry_g_ref[j] = ug[tm - SUBLANES:, :]
        carry_v_ref[j] = uv[tm - SUBLANES:, :]
    sg_ref[...] = ug[tm - SUBLANES:, :]
    sv_ref[...] = uv[tm - SUBLANES:, :]
    gate = cbg_ref[...] + cwg_ref[0:1, :] * ug2 + cwg_ref[1:2, :] * ug1 + cwg_ref[2:3, :] * ug
    val = cbv_ref[...] + cwv_ref[0:1, :] * uv2 + cwv_ref[1:2, :] * uv1 + cwv_ref[2:3, :] * uv
    h_ref[...] = (gate * jax.nn.sigmoid(gate) * val).astype(h_ref.dtype)


def _ffn_up(xb, w_up, conv_w, conv_b, seq_len, state=None):
    m, d = xb.shape
    d_ff = w_up.shape[1] // 2
    decode = state is not None
    tf = _pick(d_ff, (256, 128))
    nf = d_ff // tf
    tm = m if decode else _pick(seq_len, (1024, 512, 256, 128))
    n_tiles = m // tm
    gcol = lambda i, j: (0, j)
    vcol = lambda i, j: (0, j + nf)
    in_specs = [pl.BlockSpec((tm, d), lambda i, j: (i, 0)),
                pl.BlockSpec((d, tf), gcol), pl.BlockSpec((d, tf), vcol),
                pl.BlockSpec((CONV_W, tf), gcol), pl.BlockSpec((CONV_W, tf), vcol),
                pl.BlockSpec((1, tf), gcol), pl.BlockSpec((1, tf), vcol)]
    args = [xb, w_up, w_up, conv_w, conv_w, conv_b, conv_b]
    scratch = []
    if decode:
        s0, s1 = state[:, 0, :], state[:, 1, :]
        in_specs += [pl.BlockSpec((tm, tf), lambda i, j: (i, j)), pl.BlockSpec((tm, tf), lambda i, j: (i, j)),
                     pl.BlockSpec((tm, tf), lambda i, j: (i, j + nf)),
                     pl.BlockSpec((tm, tf), lambda i, j: (i, j + nf))]
        args += [s0, s1, s0, s1]
    else:
        scratch = [pltpu.VMEM((nf, SUBLANES, tf), F32), pltpu.VMEM((nf, SUBLANES, tf), F32)]
    return pl.pallas_call(
        functools.partial(_ffn_up_kernel, tiles_per_seq=max(seq_len // tm, 1), decode=decode),
        grid=(n_tiles, nf),
        in_specs=in_specs,
        out_specs=[pl.BlockSpec((tm, tf), lambda i, j: (i, j)),
                   pl.BlockSpec((None, SUBLANES, tf), lambda i, j: (i, 0, j)),
                   pl.BlockSpec((None, SUBLANES, tf), lambda i, j: (i, 0, j))],
        out_shape=[jax.ShapeDtypeStruct((m, d_ff), BF16),
                   jax.ShapeDtypeStruct((n_tiles, SUBLANES, d_ff), F32),
                   jax.ShapeDtypeStruct((n_tiles, SUBLANES, d_ff), F32)],
        scratch_shapes=scratch,
        compiler_params=_params("arbitrary", "arbitrary"),
        name="ffn_up_decode" if decode else "ffn_up",
    )(*args)


def _layer(x, xb, lw, *, alpha, batch, seq_len, decode_ctx=None):
    (w_in, w_f, b_f, sgu_ln_g, sgu_ln_b, sgu_w, sgu_b_t, conv_w, conv_b, mix_norm_g, w_o, ln1_g, ln1_b,
     w_up, ffn_conv_w, ffn_conv_b, w_down, ln2_g, ln2_b) = lw
    m, d = x.shape
    n_heads = d // HEAD_DIM
    d_a = (n_heads // 4) * HEAD_DIM
    d_c = d - 2 * d_a
    h_c = d_c // HEAD_DIM
    assert d_c == 2 * d_a
    decode = decode_ctx is not None

    n_zb = 2 * d_a + 3 * d_a + d_c
    zb = _matmul(xb, w_in, col_off=0, n_cols=n_zb, out_dtype=BF16, name="in_proj_mix")
    k = _matmul(xb, w_in, col_off=n_zb, n_cols=d_c, out_dtype=F32, name="in_proj_k")
    v = _matmul(xb, w_in, col_off=n_zb + d_c, n_cols=d_c, out_dtype=F32, name="in_proj_v")
    logf = _logf(xb, w_f, b_f)[:, :h_c]

    g_a, g_b, g_c = mix_norm_g[:, :d_a], mix_norm_g[:, d_a:2 * d_a], mix_norm_g[:, 2 * d_a:]

    if decode:
        zuv = jnp.zeros((m, CHUNK, 2 * d_a), BF16).at[:, 0, :].set(zb[:, :2 * d_a]).reshape(m * CHUNK, 2 * d_a)
        ya_full, vln_full = _sgu(zuv, sgu_ln_g, sgu_ln_b, sgu_w, sgu_b_t, g_a, d_a)
        ya = ya_full.reshape(m, CHUNK, d_a)[:, 0, :]
        sgu_v = vln_full.reshape(m, CHUNK, d_a)[:, :1, :]
        yb, c_rows = _bconv(zb, conv_w, conv_b, g_b, d_a, 2, 1, state=decode_ctx["conv"])
        conv_state = jnp.stack([decode_ctx["conv"][:, 1, :], c_rows[0]], axis=1)
        q3 = zb[:, 5 * d_a:].reshape(m, h_c, HEAD_DIM)
        yc = _fox_decode(q3, k.reshape(m, h_c, HEAD_DIM), v.reshape(m, h_c, HEAD_DIM), logf[:, :, None],
                         g_c.reshape(h_c, HEAD_DIM), decode_ctx["cache_k"], decode_ctx["cache_v"], decode_ctx["cache_lf_t"],
                         decode_ctx["page_table"], decode_ctx["layer"]).reshape(m, d_c)
        kh, vh = k.reshape(m, 1, h_c, HEAD_DIM), v.reshape(m, 1, h_c, HEAD_DIM)
        logf_out = logf.reshape(m, 1, h_c)
    else:
        ya, vln = _sgu(zb, sgu_ln_g, sgu_ln_b, sgu_w, sgu_b_t, g_a, d_a)
        last_chunk = ((seq_len - 1) // CHUNK) * CHUNK
        sgu_v = vln.reshape(batch, seq_len, d_a)[:, last_chunk:, :]
        yb, c_rows = _bconv(zb, conv_w, conv_b, g_b, d_a, 2, seq_len)
        tiles = c_rows.shape[0] // batch
        conv_state = c_rows.reshape(batch, tiles, SUBLANES, d_a)[:, -1, SUBLANES - (CONV_W - 1):, :]
        logf_b = logf.reshape(batch, seq_len, h_c)
        f_cum = _cumsum_time(jnp.transpose(logf_b, (0, 2, 1)))
        yc = _fox_prompt(zb, k, v, f_cum[:, :, :, None], f_cum[:, :, None, :], g_c,
                         batch=batch, seq_len=seq_len, n_heads=h_c, q_col_blk=5 * d_a // HEAD_DIM)
        kh, vh = k.reshape(batch, seq_len, h_c, HEAD_DIM), v.reshape(batch, seq_len, h_c, HEAD_DIM)
        logf_out = logf_b

    h = _oproj(ya, yb, yc, w_o)
    x1, x1b = _ln_res(x, h, ln1_g, ln1_b, alpha)

    if decode:
        hb, up_g, up_v = _ffn_up(x1b, w_up, ffn_conv_w, ffn_conv_b, 1, state=decode_ctx["ffn"])
        up_rows = jnp.concatenate([up_g[0], up_v[0]], axis=-1)
        ffn_state = jnp.stack([decode_ctx["ffn"][:, 1, :], up_rows], axis=1)
    else:
        hb, up_g, up_v = _ffn_up(x1b, w_up, ffn_conv_w, ffn_conv_b, seq_len)
        tiles = up_g.shape[0] // batch
        up_rows = jnp.concatenate([up_g, up_v], axis=-1)
        ffn_state = up_rows.reshape(batch, tiles, SUBLANES, -1)[:, -1, SUBLANES - (CONV_W - 1):, :]
    ff = _matmul(hb, w_down, out_dtype=F32, name="down_proj")
    x2, x2b = _ln_res(x1, ff, ln2_g, ln2_b, alpha)
    return x2, x2b, (kh, vh, logf_out, conv_state, sgu_v, ffn_state)


def kernel(x_prompt, x_sample, cache_k, cache_v, cache_logf, page_table, state_conv, state_ffn_conv,
           w_in, b_f, sgu_ln_g, sgu_ln_b, sgu_w, sgu_b, conv_w, conv_b, mix_norm_g, w_o,
           ln1_g, ln1_b, w_up, ffn_conv_w, ffn_conv_b, w_down, ln2_g, ln2_b):
    depth, d, d_in = w_in.shape
    batch, seq_len, _ = x_prompt.shape
    dec_batch, dec_seq, _ = x_sample.shape
    assert dec_seq == 1, "the sample group is one new token per sequence"
    alpha = (2 * depth) ** 0.25
    h_c = b_f.shape[1]
    n_main = d_in - h_c

    w_in_b = w_in.astype(BF16)
    w_f = jnp.pad(w_in_b[:, :, n_main:], ((0, 0), (0, 0), (0, HEAD_DIM - h_c)))
    b_f_p = jnp.pad(b_f, ((0, 0), (0, HEAD_DIM - h_c)))[:, None, :]
    w_o_b, w_up_b, w_down_b = w_o.astype(BF16), w_up.astype(BF16), w_down.astype(BF16)
    sgu_b_t = jnp.transpose(sgu_b, (0, 2, 1))
    cache_lf_t = jnp.transpose(cache_logf, (0, 1, 3, 2))
    row = lambda a: a[:, None, :]

    xp = x_prompt.reshape(batch * seq_len, d)
    xs = x_sample.reshape(dec_batch, d)
    xpb, xsb = xp.astype(BF16), xs.astype(BF16)
    sp, ss = [], []
    for l in range(depth):
        lw = (w_in_b[l], w_f[l], b_f_p[l], row(sgu_ln_g)[l], row(sgu_ln_b)[l], sgu_w[l], sgu_b_t[l],
              conv_w[l], row(conv_b)[l], row(mix_norm_g)[l], w_o_b[l], row(ln1_g)[l], row(ln1_b)[l],
              w_up_b[l], ffn_conv_w[l], row(ffn_conv_b)[l], w_down_b[l], row(ln2_g)[l], row(ln2_b)[l])
        xp, xpb, st = _layer(xp, xpb, lw, alpha=alpha, batch=batch, seq_len=seq_len)
        sp.append(st)
        ctx = dict(conv=state_conv[l], ffn=state_ffn_conv[l], cache_k=cache_k, cache_v=cache_v,
                   cache_lf_t=cache_lf_t, page_table=page_table, layer=l)
        xs, xsb, st = _layer(xs, xsb, lw, alpha=alpha, batch=dec_batch, seq_len=1, decode_ctx=ctx)
        ss.append(st)

    def stack(sts, i):
        return jnp.stack([s[i] for s in sts])

    return (xp.reshape(batch, seq_len, d), xs.reshape(dec_batch, 1, d),
            stack(sp, 0), stack(sp, 1), stack(sp, 2), stack(sp, 3), stack(sp, 5), stack(sp, 4),
            stack(ss, 0), stack(ss, 1), stack(ss, 2), stack(ss, 3), stack(ss, 5), stack(ss, 4))
```

```python
import functools

import jax
import jax.numpy as jnp
from jax import lax
from jax.experimental import pallas as pl
from jax.experimental.pallas import tpu as pltpu

HEAD_DIM = 128
CHUNK = 128
CONV_W = 3
EPS = 1e-5
MASK_VALUE = -1e30
SUBLANES = 8
VMEM_LIMIT_BYTES = 56 * 1024 * 1024
PAGES_PER_STEP = 8

F32 = jnp.float32
BF16 = jnp.bfloat16


def _pick(dim, candidates):
    for c in candidates:
        if dim % c == 0:
            return c
    return dim


def _params(*semantics):
    return pltpu.CompilerParams(dimension_semantics=semantics, vmem_limit_bytes=VMEM_LIMIT_BYTES)


def _mm_kernel(x_ref, w_ref, o_ref, *acc, nk):
    if nk == 1:
        o_ref[...] = jnp.dot(x_ref[...], w_ref[...], preferred_element_type=F32).astype(o_ref.dtype)
        return
    acc_ref, = acc
    k = pl.program_id(2)

    @pl.when(k == 0)
    def _():
        acc_ref[...] = jnp.dot(x_ref[...], w_ref[...], preferred_element_type=F32)

    @pl.when(k > 0)
    def _():
        acc_ref[...] += jnp.dot(x_ref[...], w_ref[...], preferred_element_type=F32)

    @pl.when(k == nk - 1)
    def _():
        o_ref[...] = acc_ref[...].astype(o_ref.dtype)


def _matmul(x, w, *, col_off=0, n_cols=None, out_dtype=F32, name="mm"):
    m, kdim = x.shape
    n_cols = w.shape[1] if n_cols is None else n_cols
    tm = _pick(m, (1024, 512, 256, 128))
    tk = kdim if kdim <= 4096 else _pick(kdim, (5504, 4096, 2048, 1024, 512, 256, 128))
    tn_cands = (512, 256, 128) if tk > 4096 else (1024, 512, 256, 128)
    tn = next(c for c in tn_cands if n_cols % c == 0 and col_off % c == 0)
    joff = col_off // tn
    nk = kdim // tk
    grid = (m // tm, n_cols // tn, nk)
    return pl.pallas_call(
        functools.partial(_mm_kernel, nk=nk),
        grid=grid,
        in_specs=[pl.BlockSpec((tm, tk), lambda i, j, k: (i, k)),
                  pl.BlockSpec((tk, tn), lambda i, j, k: (k, j + joff))],
        out_specs=pl.BlockSpec((tm, tn), lambda i, j, k: (i, j)),
        out_shape=jax.ShapeDtypeStruct((m, n_cols), out_dtype),
        scratch_shapes=[pltpu.VMEM((tm, tn), F32)] if nk > 1 else [],
        compiler_params=_params("parallel", "parallel", "arbitrary"),
        name=name,
    )(x, w)


def _logf_kernel(x_ref, w_ref, b_ref, o_ref):
    z = jnp.dot(x_ref[...], w_ref[...], preferred_element_type=F32) + b_ref[...]
    o_ref[...] = jnp.minimum(z, 0.0) - jnp.log1p(jnp.exp(-jnp.abs(z)))


def _logf(x, w_f, b_f):
    m, kdim = x.shape
    tm = _pick(m, (512, 256, 128))
    return pl.pallas_call(
        _logf_kernel,
        grid=(m // tm,),
        in_specs=[pl.BlockSpec((tm, kdim), lambda i: (i, 0)),
                  pl.BlockSpec((kdim, HEAD_DIM), lambda i: (0, 0)),
                  pl.BlockSpec((1, HEAD_DIM), lambda i: (0, 0))],
        out_specs=pl.BlockSpec((tm, HEAD_DIM), lambda i: (i, 0)),
        out_shape=jax.ShapeDtypeStruct((m, HEAD_DIM), F32),
        compiler_params=_params("parallel"),
        name="logf",
    )(x, w_f, b_f)


def _head_rms(y, g):
    return y * lax.rsqrt(jnp.mean(y * y, axis=-1, keepdims=True) + EPS) * g


def _shift_rows(c, prev):
    row = lax.broadcasted_iota(jnp.int32, c.shape, 0)
    p1 = prev[SUBLANES - 1:SUBLANES, :]
    p2 = prev[SUBLANES - 2:SUBLANES - 1, :]
    c1 = jnp.where(row == 0, p1, pltpu.roll(c, 1, axis=0))
    c2 = jnp.where(row == 0, p2, jnp.where(row == 1, p1, pltpu.roll(c, 2, axis=0)))
    return c1, c2


def _sgu_kernel(u_ref, v_ref, lng_ref, lnb_ref, w_ref, bst_ref, g_ref, y_ref, vln_ref, *, n_chunks, n_heads):
    u = jax.nn.gelu(u_ref[...].astype(F32))
    v = jax.nn.gelu(v_ref[...].astype(F32))
    mu = jnp.mean(v, axis=-1, keepdims=True)
    d = v - mu
    var = jnp.mean(d * d, axis=-1, keepdims=True)
    vln = d * lax.rsqrt(var + EPS) * lng_ref[...] + lnb_ref[...]
    vln_ref[...] = vln
    vb = vln.astype(BF16)
    t_idx = lax.broadcasted_iota(jnp.int32, (CHUNK, CHUNK), 0)
    s_idx = lax.broadcasted_iota(jnp.int32, (CHUNK, CHUNK), 1)
    causal = s_idx <= t_idx
    for h in range(n_heads):
        cols = slice(h * HEAD_DIM, (h + 1) * HEAD_DIM)
        wm = jnp.where(causal, w_ref[h], 0.0).astype(BF16)
        bias = bst_ref[:, h:h + 1]
        g = g_ref[:, cols]
        for c in range(n_chunks):
            rows = slice(c * CHUNK, (c + 1) * CHUNK)
            mixed = jnp.dot(wm, vb[rows, cols], preferred_element_type=F32) + bias
            y_ref[rows, cols] = _head_rms(u[rows, cols] * mixed, g).astype(y_ref.dtype)


def _sgu(zb, ln_g, ln_b, w_s, b_s_t, norm_g, d_a):
    m = zb.shape[0]
    n_heads = d_a // HEAD_DIM
    tm = _pick(m, (256, 128))
    return pl.pallas_call(
        functools.partial(_sgu_kernel, n_chunks=tm // CHUNK, n_heads=n_heads),
        grid=(m // tm,),
        in_specs=[pl.BlockSpec((tm, d_a), lambda i: (i, 0)),
                  pl.BlockSpec((tm, d_a), lambda i: (i, 1)),
                  pl.BlockSpec((1, d_a), lambda i: (0, 0)),
                  pl.BlockSpec((1, d_a), lambda i: (0, 0)),
                  pl.BlockSpec((n_heads, CHUNK, CHUNK), lambda i: (0, 0, 0)),
                  pl.BlockSpec((CHUNK, n_heads), lambda i: (0, 0)),
                  pl.BlockSpec((1, d_a), lambda i: (0, 0))],
        out_specs=[pl.BlockSpec((tm, d_a), lambda i: (i, 0)),
                   pl.BlockSpec((tm, d_a), lambda i: (i, 0))],
        out_shape=[jax.ShapeDtypeStruct((m, d_a), BF16), jax.ShapeDtypeStruct((m, d_a), F32)],
        compiler_params=_params("parallel"),
        name="sgu",
    )(zb, zb, ln_g, ln_b, w_s, b_s_t, norm_g)


def _bconv_kernel(*refs, tiles_per_seq, n_heads, decode):
    if decode:
        x_ref, gb_ref, gc_ref, cw_ref, cb_ref, g_ref, p2_ref, p1_ref, y_ref, st_ref = refs
    else:
        x_ref, gb_ref, gc_ref, cw_ref, cb_ref, g_ref, y_ref, st_ref, carry_ref = refs
    c = gc_ref[...].astype(F32) * x_ref[...].astype(F32)
    tm = c.shape[0]
    if decode:
        c1, c2 = p1_ref[...], p2_ref[...]
    else:
        start = pl.program_id(0) % tiles_per_seq == 0
        prev = jnp.where(start, 0.0, carry_ref[...])
        c1, c2 = _shift_rows(c, prev)
        carry_ref[...] = c[tm - SUBLANES:, :]
    st_ref[...] = c[tm - SUBLANES:, :]
    conv = cb_ref[...] + cw_ref[0:1, :] * c2 + cw_ref[1:2, :] * c1 + cw_ref[2:3, :] * c
    y = gb_ref[...].astype(F32) * conv
    for h in range(n_heads):
        cols = slice(h * HEAD_DIM, (h + 1) * HEAD_DIM)
        y_ref[:, cols] = _head_rms(y[:, cols], g_ref[:, cols]).astype(y_ref.dtype)


def _bconv(zb, conv_w, conv_b, norm_g, d_b, col_blk, seq_len, state=None):
    m = zb.shape[0]
    decode = state is not None
    tm = m if decode else _pick(seq_len, (512, 256, 128))
    n_tiles = m // tm
    row = lambda i: (i, 0)
    fixed = lambda i: (0, 0)
    in_specs = [pl.BlockSpec((tm, d_b), lambda i: (i, col_blk)),
                pl.BlockSpec((tm, d_b), lambda i: (i, col_blk + 1)),
                pl.BlockSpec((tm, d_b), lambda i: (i, col_blk + 2)),
                pl.BlockSpec((CONV_W, d_b), fixed),
                pl.BlockSpec((1, d_b), fixed),
                pl.BlockSpec((1, d_b), fixed)]
    args = [zb, zb, zb, conv_w, conv_b, norm_g]
    scratch = []
    if decode:
        in_specs += [pl.BlockSpec((tm, d_b), row), pl.BlockSpec((tm, d_b), row)]
        args += [state[:, 0, :], state[:, 1, :]]
    else:
        scratch = [pltpu.VMEM((SUBLANES, d_b), F32)]
    return pl.pallas_call(
        functools.partial(_bconv_kernel, tiles_per_seq=max(seq_len // tm, 1), n_heads=d_b // HEAD_DIM,
                          decode=decode),
        grid=(n_tiles,),
        in_specs=in_specs,
        out_specs=[pl.BlockSpec((tm, d_b), row),
                   pl.BlockSpec((None, SUBLANES, d_b), lambda i: (i, 0, 0))],
        out_shape=[jax.ShapeDtypeStruct((m, d_b), BF16),
                   jax.ShapeDtypeStruct((n_tiles, SUBLANES, d_b), F32)],
        scratch_shapes=scratch,
        compiler_params=_params("arbitrary"),
        name="bconv_decode" if decode else "bconv",
    )(*args)


def _cumsum_kernel(x_ref, o_ref):
    x = x_ref[...]
    t = x.shape[-1]
    lane = lax.broadcasted_iota(jnp.int32, x.shape, 1)
    shift = 1
    while shift < t:
        x = x + jnp.where(lane >= shift, pltpu.roll(x, shift, axis=1), 0.0)
        shift *= 2
    o_ref[...] = x


def _cumsum_time(logf_t):
    b, h, t = logf_t.shape
    return pl.pallas_call(
        _cumsum_kernel,
        grid=(b,),
        in_specs=[pl.BlockSpec((None, h, t), lambda i: (i, 0, 0))],
        out_specs=pl.BlockSpec((None, h, t), lambda i: (i, 0, 0)),
        out_shape=jax.ShapeDtypeStruct((b, h, t), F32),
        compiler_params=_params("parallel"),
        name="cumsum_logf",
    )(logf_t)


def _fox_prompt_kernel(q_ref, k_ref, v_ref, fq_ref, fk_ref, g_ref, o_ref, *, tq, nq):
    scale = HEAD_DIM ** -0.5
    kb = k_ref[...].astype(BF16)
    vb = v_ref[...].astype(BF16)
    causal = (lax.broadcasted_iota(jnp.int32, (tq, tq), 1) <= lax.broadcasted_iota(jnp.int32, (tq, tq), 0))
    contract_last = (((1,), (1,)), ((), ()))
    for qi in range(nq):
        rows = slice(qi * tq, (qi + 1) * tq)
        q = q_ref[rows, :]
        fq = fq_ref[rows, :]
        sd = lax.dot_general(q, kb[rows, :], contract_last, preferred_element_type=F32)
        sd = jnp.where(causal, sd * scale + fq - fk_ref[:, rows], MASK_VALUE)
        m = jnp.max(sd, axis=-1, keepdims=True)
        if qi > 0:
            past = slice(0, qi * tq)
            so = lax.dot_general(q, kb[past, :], contract_last, preferred_element_type=F32)
            so = so * scale + fq - fk_ref[:, past]
            m = jnp.maximum(m, jnp.max(so, axis=-1, keepdims=True))
            po = jnp.exp(so - m)
            l = jnp.sum(po, axis=-1, keepdims=True)
            acc = jnp.dot(po.astype(BF16), vb[past, :], preferred_element_type=F32)
        pd = jnp.exp(sd - m)
        pv = jnp.dot(pd.astype(BF16), vb[rows, :], preferred_element_type=F32)
        if qi > 0:
            l = l + jnp.sum(pd, axis=-1, keepdims=True)
            acc = acc + pv
        else:
            l = jnp.sum(pd, axis=-1, keepdims=True)
            acc = pv
        o_ref[rows, :] = _head_rms(acc / l, g_ref[...]).astype(o_ref.dtype)


def _fox_prompt(zb, k, v, f_col, f_row, norm_g, *, batch, seq_len, n_heads, q_col_blk):
    m = zb.shape[0]
    tq = _pick(seq_len, (512, 256, 128))
    seq_blk = lambda b, h: (b, h)
    return pl.pallas_call(
        functools.partial(_fox_prompt_kernel, tq=tq, nq=seq_len // tq),
        grid=(batch, n_heads),
        in_specs=[pl.BlockSpec((seq_len, HEAD_DIM), lambda b, h: (b, q_col_blk + h)),
                  pl.BlockSpec((seq_len, HEAD_DIM), seq_blk),
                  pl.BlockSpec((seq_len, HEAD_DIM), seq_blk),
                  pl.BlockSpec((None, None, seq_len, 1), lambda b, h: (b, h, 0, 0)),
                  pl.BlockSpec((None, None, 1, seq_len), lambda b, h: (b, h, 0, 0)),
                  pl.BlockSpec((1, HEAD_DIM), lambda b, h: (0, h))],
        out_specs=pl.BlockSpec((seq_len, HEAD_DIM), seq_blk),
        out_shape=jax.ShapeDtypeStruct((m, n_heads * HEAD_DIM), BF16),
        compiler_params=_params("parallel", "parallel"),
        name="fox_prompt",
    )(zb, k, v, f_col, f_row, norm_g)


def _fox_decode_kernel(pt_ref, q_ref, kn_ref, vn_ref, lfn_ref, g_ref, *rest, n_heads, page):
    del pt_ref
    pps = PAGES_PER_STEP
    k_refs, v_refs, lf_refs = rest[:pps], rest[pps:2 * pps], rest[2 * pps:3 * pps]
    o_ref, m_ref, l_ref, acc_ref, carry_ref = rest[3 * pps:]
    step = pl.program_id(1)
    scale = HEAD_DIM ** -0.5
    n_rows = page * n_heads

    @pl.when(step == 0)
    def _():
        m_ref[...] = jnp.full(m_ref.shape, MASK_VALUE, F32)
        l_ref[...] = jnp.zeros(l_ref.shape, F32)
        acc_ref[...] = jnp.zeros(acc_ref.shape, F32)
        carry_ref[...] = jnp.zeros(carry_ref.shape, F32)

    q = q_ref[...]
    lf = jnp.concatenate([r[...] for r in lf_refs], axis=0)
    lane = lax.broadcasted_iota(jnp.int32, lf.shape, 1)
    suffix, total = lf, lf
    shift = n_heads
    while shift < n_rows:
        suffix = suffix + jnp.where(lane < n_rows - shift, pltpu.roll(suffix, n_rows - shift, axis=1), 0.0)
        total = total + pltpu.roll(total, shift, axis=1)
        shift *= 2
    carry = carry_ref[...]
    lf_new = lfn_ref[...]
    bias = [None] * pps
    for r in reversed(range(pps)):
        bias[r] = suffix[r:r + 1, :] - lf[r:r + 1, :] + carry + lf_new
        carry = carry + total[r:r + 1, :]
    carry_ref[...] = carry

    own_head = ((lax.broadcasted_iota(jnp.int32, (n_heads, n_rows), 1) & (n_heads - 1))
                == lax.broadcasted_iota(jnp.int32, (n_heads, n_rows), 0))
    contract_last = (((1,), (1,)), ((), ()))
    s = [jnp.where(own_head,
                   lax.dot_general(q, k_refs[r][...].astype(BF16), contract_last,
                                   preferred_element_type=F32) * scale + bias[r],
                   MASK_VALUE) for r in range(pps)]
    m_prev = m_ref[...]
    m_new = m_prev
    for r in range(pps):
        m_new = jnp.maximum(m_new, jnp.max(s[r], axis=-1, keepdims=True))
    alpha = jnp.exp(m_prev - m_new)
    l_new = alpha * l_ref[...]
    acc = alpha * acc_ref[...]
    for r in range(pps):
        p = jnp.exp(s[r] - m_new)
        l_new = l_new + jnp.sum(p, axis=-1, keepdims=True)
        acc = acc + jnp.dot(p.astype(BF16), v_refs[r][...].astype(BF16), preferred_element_type=F32)
    l_ref[...] = l_new
    acc_ref[...] = acc
    m_ref[...] = m_new

    @pl.when(step == pl.num_programs(1) - 1)
    def _():
        kn = kn_ref[...].astype(BF16).astype(F32)
        vn = vn_ref[...].astype(BF16).astype(F32)
        s_self = jnp.sum(q.astype(F32) * kn, axis=-1, keepdims=True) * scale
        m_prev = m_ref[...]
        m_fin = jnp.maximum(m_prev, s_self)
        a = jnp.exp(m_prev - m_fin)
        p_self = jnp.exp(s_self - m_fin)
        l_fin = a * l_ref[...] + p_self
        o = (a * acc_ref[...] + p_self.astype(BF16).astype(F32) * vn) / l_fin
        o_ref[...] = _head_rms(o, g_ref[...]).astype(o_ref.dtype)


def _fox_decode(q, k_new, v_new, lf_new, norm_g, cache_k, cache_v, cache_lf, page_table, layer):
    bsz, n_heads, _ = q.shape
    n_rows = cache_k.shape[2]
    page = n_rows // n_heads
    n_pages = page_table.shape[1]
    pps = PAGES_PER_STEP
    assert n_pages % pps == 0 and n_heads & (n_heads - 1) == 0
    n_steps = n_pages // pps

    def page_idx(b, s, pt, r):
        return pt[b, (n_steps - 1 - s) * pps + r]

    per_b = lambda b, s, pt: (b, 0, 0)
    kv_spec = lambda r: pl.BlockSpec((None, None, n_rows, HEAD_DIM),
                                     lambda b, s, pt: (layer, page_idx(b, s, pt, r), 0, 0))
    lf_spec = lambda r: pl.BlockSpec((None, None, 1, n_rows),
                                     lambda b, s, pt: (layer, page_idx(b, s, pt, r), 0, 0))
    in_specs = ([pl.BlockSpec((None, n_heads, HEAD_DIM), per_b)] * 3
                + [pl.BlockSpec((None, 1, n_rows), per_b),
                   pl.BlockSpec((n_heads, HEAD_DIM), lambda b, s, pt: (0, 0))]
                + [kv_spec(r) for r in range(pps)] * 2
                + [lf_spec(r) for r in range(pps)])
    grid_spec = pltpu.PrefetchScalarGridSpec(
        num_scalar_prefetch=1,
        grid=(bsz, n_steps),
        in_specs=in_specs,
        out_specs=pl.BlockSpec((None, n_heads, HEAD_DIM), per_b),
        scratch_shapes=[pltpu.VMEM((n_heads, 1), F32), pltpu.VMEM((n_heads, 1), F32),
                        pltpu.VMEM((n_heads, HEAD_DIM), F32), pltpu.VMEM((1, n_rows), F32)])
    return pl.pallas_call(
        functools.partial(_fox_decode_kernel, n_heads=n_heads, page=page),
        grid_spec=grid_spec,
        out_shape=jax.ShapeDtypeStruct((bsz, n_heads, HEAD_DIM), BF16),
        compiler_params=_params("parallel", "arbitrary"),
        name="fox_decode",
    )(page_table, q, k_new, v_new, lf_new, norm_g,
      *([cache_k] * pps), *([cache_v] * pps), *([cache_lf] * pps))


def _oproj_kernel(a_ref, b_ref, c_ref, wa_ref, wb_ref, wc_ref, o_ref):
    acc = jnp.dot(a_ref[...], wa_ref[...], preferred_element_type=F32)
    acc += jnp.dot(b_ref[...], wb_ref[...], preferred_element_type=F32)
    acc += jnp.dot(c_ref[...], wc_ref[...], preferred_element_type=F32)
    o_ref[...] = acc


def _oproj(ya, yb, yc, w_o):
    m, d_a = ya.shape
    d_c = yc.shape[1]
    d = w_o.shape[1]
    assert yb.shape[1] == d_a and d_c == 2 * d_a and w_o.shape[0] == 2 * d_a + d_c
    tm = _pick(m, (1024, 512, 256, 128))
    tn = _pick(d, (1024, 512, 256, 128))
    row = lambda i, j: (i, 0)
    return pl.pallas_call(
        _oproj_kernel,
        grid=(m // tm, d // tn),
        in_specs=[pl.BlockSpec((tm, d_a), row), pl.BlockSpec((tm, d_a), row), pl.BlockSpec((tm, d_c), row),
                  pl.BlockSpec((d_a, tn), lambda i, j: (0, j)),
                  pl.BlockSpec((d_a, tn), lambda i, j: (1, j)),
                  pl.BlockSpec((d_c, tn), lambda i, j: (1, j))],
        out_specs=pl.BlockSpec((tm, tn), lambda i, j: (i, j)),
        out_shape=jax.ShapeDtypeStruct((m, d), F32),
        compiler_params=_params("parallel", "parallel"),
        name="oproj",
    )(ya, yb, yc, w_o, w_o, w_o)


def _ln_res_kernel(x_ref, h_ref, g_ref, b_ref, o_ref, ob_ref, *, alpha):
    y = alpha * x_ref[...] + h_ref[...]
    mu = jnp.mean(y, axis=-1, keepdims=True)
    d = y - mu
    var = jnp.mean(d * d, axis=-1, keepdims=True)
    o = d * lax.rsqrt(var + EPS) * g_ref[...] + b_ref[...]
    o_ref[...] = o
    ob_ref[...] = o.astype(ob_ref.dtype)


def _ln_res(x, h, g, b, alpha):
    m, d = x.shape
    tm = _pick(m, (256, 128))
    row = lambda i: (i, 0)
    fixed = lambda i: (0, 0)
    return pl.pallas_call(
        functools.partial(_ln_res_kernel, alpha=alpha),
        grid=(m // tm,),
        in_specs=[pl.BlockSpec((tm, d), row), pl.BlockSpec((tm, d), row),
                  pl.BlockSpec((1, d), fixed), pl.BlockSpec((1, d), fixed)],
        out_specs=[pl.BlockSpec((tm, d), row), pl.BlockSpec((tm, d), row)],
        out_shape=[jax.ShapeDtypeStruct((m, d), F32), jax.ShapeDtypeStruct((m, d), BF16)],
        compiler_params=_params("parallel"),
        name="ln_res",
    )(x, h, g, b)


def _conv_taps(c2, c1, c, cw_ref, cb_ref):
    return cb_ref[...] + cw_ref[0:1, :] * c2 + cw_ref[1:2, :] * c1 + cw_ref[2:3, :] * c


def _ffn_up_kernel(x_ref, wg_ref, wv_ref, cwg_ref, cwv_ref, cbg_ref, cbv_ref, h_ref, sg_ref, sv_ref,
                   raw_g_ref, raw_v_ref, carry_g_ref, carry_v_ref, *, tiles_per_seq, nf, n_tiles):
    s = pl.program_id(0)
    tm = x_ref.shape[0]

    @pl.when(s == 0)
    def _():
        raw_g_ref[...] = jnp.zeros(raw_g_ref.shape, F32)
        raw_v_ref[...] = jnp.zeros(raw_v_ref.shape, F32)

    def taps(raw_ref, cw_ref, cb_ref):
        return _conv_taps(raw_ref[pl.ds(SUBLANES - 2, tm), :], raw_ref[pl.ds(SUBLANES - 1, tm), :],
                          raw_ref[pl.ds(SUBLANES, tm), :], cw_ref, cb_ref)

    gate = taps(raw_g_ref, cwg_ref, cbg_ref)
    val = taps(raw_v_ref, cwv_ref, cbv_ref)
    h_ref[...] = (gate * jax.nn.sigmoid(gate) * val).astype(h_ref.dtype)
    sg_ref[...] = raw_g_ref[pl.ds(tm, SUBLANES), :]
    sv_ref[...] = raw_v_ref[pl.ds(tm, SUBLANES), :]

    sc = jnp.minimum(s, n_tiles * nf - 1)
    j = sc % nf
    start = (sc // nf) % tiles_per_seq == 0
    x = x_ref[...]
    for w_ref, raw_ref, carry_ref in ((wg_ref, raw_g_ref, carry_g_ref), (wv_ref, raw_v_ref, carry_v_ref)):
        u = jnp.dot(x, w_ref[...], preferred_element_type=F32)
        raw_ref[pl.ds(0, SUBLANES), :] = jnp.where(start, 0.0, carry_ref[j])
        raw_ref[pl.ds(SUBLANES, tm), :] = u
        carry_ref[j] = u[tm - SUBLANES:, :]


def _ffn_up(xb, w_up, conv_w, conv_b, seq_len):
    m, d = xb.shape
    d_ff = w_up.shape[1] // 2
    tf = _pick(d_ff, (256, 128))
    nf = d_ff // tf
    tm = _pick(seq_len, (1024, 512, 256, 128))
    n_tiles = m // tm
    last = n_tiles * nf - 1
    cur = lambda s: jnp.minimum(s, last)
    prev = lambda s: jnp.maximum(s - 1, 0)
    return pl.pallas_call(
        functools.partial(_ffn_up_kernel, tiles_per_seq=seq_len // tm, nf=nf, n_tiles=n_tiles),
        grid=(n_tiles * nf + 1,),
        in_specs=[pl.BlockSpec((tm, d), lambda s: (cur(s) // nf, 0)),
                  pl.BlockSpec((d, tf), lambda s: (0, cur(s) % nf)),
                  pl.BlockSpec((d, tf), lambda s: (0, cur(s) % nf + nf)),
                  pl.BlockSpec((CONV_W, tf), lambda s: (0, prev(s) % nf)),
                  pl.BlockSpec((CONV_W, tf), lambda s: (0, prev(s) % nf + nf)),
                  pl.BlockSpec((1, tf), lambda s: (0, prev(s) % nf)),
                  pl.BlockSpec((1, tf), lambda s: (0, prev(s) % nf + nf))],
        out_specs=[pl.BlockSpec((tm, tf), lambda s: (prev(s) // nf, prev(s) % nf)),
                   pl.BlockSpec((None, SUBLANES, tf), lambda s: (prev(s) // nf, 0, prev(s) % nf)),
                   pl.BlockSpec((None, SUBLANES, tf), lambda s: (prev(s) // nf, 0, prev(s) % nf))],
        out_shape=[jax.ShapeDtypeStruct((m, d_ff), BF16),
                   jax.ShapeDtypeStruct((n_tiles, SUBLANES, d_ff), F32),
                   jax.ShapeDtypeStruct((n_tiles, SUBLANES, d_ff), F32)],
        scratch_shapes=[pltpu.VMEM((tm + SUBLANES, tf), F32), pltpu.VMEM((tm + SUBLANES, tf), F32),
                        pltpu.VMEM((nf, SUBLANES, tf), F32), pltpu.VMEM((nf, SUBLANES, tf), F32)],
        compiler_params=_params("arbitrary"),
        name="ffn_up",
    )(xb, w_up, w_up, conv_w, conv_w, conv_b, conv_b)


def _ffn_up_decode_kernel(x_ref, wg_ref, wv_ref, cwg_ref, cwv_ref, cbg_ref, cbv_ref,
                          pg2_ref, pg1_ref, pv2_ref, pv1_ref, h_ref, ug_ref, uv_ref):
    x = x_ref[...]
    ug = jnp.dot(x, wg_ref[...], preferred_element_type=F32)
    uv = jnp.dot(x, wv_ref[...], preferred_element_type=F32)
    ug_ref[...] = ug
    uv_ref[...] = uv
    gate = _conv_taps(pg2_ref[...], pg1_ref[...], ug, cwg_ref, cbg_ref)
    val = _conv_taps(pv2_ref[...], pv1_ref[...], uv, cwv_ref, cbv_ref)
    h_ref[...] = (gate * jax.nn.sigmoid(gate) * val).astype(h_ref.dtype)


def _ffn_up_decode(xb, w_up, conv_w, conv_b, state):
    m, d = xb.shape
    d_ff = w_up.shape[1] // 2
    tf = _pick(d_ff, (256, 128))
    nf = d_ff // tf
    gcol = lambda j: (0, j)
    vcol = lambda j: (0, j + nf)
    s0, s1 = state[:, 0, :], state[:, 1, :]
    return pl.pallas_call(
        _ffn_up_decode_kernel,
        grid=(nf,),
        in_specs=[pl.BlockSpec((m, d), lambda j: (0, 0)),
                  pl.BlockSpec((d, tf), gcol), pl.BlockSpec((d, tf), vcol),
                  pl.BlockSpec((CONV_W, tf), gcol), pl.BlockSpec((CONV_W, tf), vcol),
                  pl.BlockSpec((1, tf), gcol), pl.BlockSpec((1, tf), vcol),
                  pl.BlockSpec((m, tf), gcol), pl.BlockSpec((m, tf), gcol),
                  pl.BlockSpec((m, tf), vcol), pl.BlockSpec((m, tf), vcol)],
        out_specs=[pl.BlockSpec((m, tf), gcol), pl.BlockSpec((m, tf), gcol), pl.BlockSpec((m, tf), gcol)],
        out_shape=[jax.ShapeDtypeStruct((m, d_ff), BF16),
                   jax.ShapeDtypeStruct((m, d_ff), F32), jax.ShapeDtypeStruct((m, d_ff), F32)],
        compiler_params=_params("parallel"),
        name="ffn_up_decode",
    )(xb, w_up, w_up, conv_w, conv_w, conv_b, conv_b, s0, s1, s0, s1)


def _layer(x, xb, lw, *, alpha, batch, seq_len, decode_ctx=None):
    (w_in, w_f, b_f, sgu_ln_g, sgu_ln_b, sgu_w, sgu_b_t, conv_w, conv_b, mix_norm_g, w_o, ln1_g, ln1_b,
     w_up, ffn_conv_w, ffn_conv_b, w_down, ln2_g, ln2_b) = lw
    m, d = x.shape
    n_heads = d // HEAD_DIM
    d_a = (n_heads // 4) * HEAD_DIM
    d_c = d - 2 * d_a
    h_c = d_c // HEAD_DIM
    assert d_c == 2 * d_a
    decode = decode_ctx is not None

    n_zb = 2 * d_a + 3 * d_a + d_c
    zb = _matmul(xb, w_in, col_off=0, n_cols=n_zb, out_dtype=BF16, name="in_proj_mix")
    k = _matmul(xb, w_in, col_off=n_zb, n_cols=d_c, out_dtype=F32, name="in_proj_k")
    v = _matmul(xb, w_in, col_off=n_zb + d_c, n_cols=d_c, out_dtype=F32, name="in_proj_v")
    logf = _logf(xb, w_f, b_f)[:, :h_c]

    g_a, g_b, g_c = mix_norm_g[:, :d_a], mix_norm_g[:, d_a:2 * d_a], mix_norm_g[:, 2 * d_a:]

    if decode:
        zuv = jnp.zeros((m, CHUNK, 2 * d_a), BF16).at[:, 0, :].set(zb[:, :2 * d_a]).reshape(m * CHUNK, 2 * d_a)
        ya_full, vln_full = _sgu(zuv, sgu_ln_g, sgu_ln_b, sgu_w, sgu_b_t, g_a, d_a)
        ya = ya_full.reshape(m, CHUNK, d_a)[:, 0, :]
        sgu_v = vln_full.reshape(m, CHUNK, d_a)[:, :1, :]
        yb, c_rows = _bconv(zb, conv_w, conv_b, g_b, d_a, 2, 1, state=decode_ctx["conv"])
        conv_state = jnp.stack([decode_ctx["conv"][:, 1, :], c_rows[0]], axis=1)
        q3 = zb[:, 5 * d_a:].reshape(m, h_c, HEAD_DIM)
        page = decode_ctx["cache_k"].shape[2] // h_c
        lf_rows = jnp.tile(logf, (1, page))[:, None, :]
        yc = _fox_decode(q3, k.reshape(m, h_c, HEAD_DIM), v.reshape(m, h_c, HEAD_DIM), lf_rows,
                         g_c.reshape(h_c, HEAD_DIM), decode_ctx["cache_k"], decode_ctx["cache_v"],
                         decode_ctx["cache_lf"], decode_ctx["page_table"], decode_ctx["layer"]).reshape(m, d_c)
        kh, vh = k.reshape(m, 1, h_c, HEAD_DIM), v.reshape(m, 1, h_c, HEAD_DIM)
        logf_out = logf.reshape(m, 1, h_c)
    else:
        ya, vln = _sgu(zb, sgu_ln_g, sgu_ln_b, sgu_w, sgu_b_t, g_a, d_a)
        last_chunk = ((seq_len - 1) // CHUNK) * CHUNK
        sgu_v = vln.reshape(batch, seq_len, d_a)[:, last_chunk:, :]
        yb, c_rows = _bconv(zb, conv_w, conv_b, g_b, d_a, 2, seq_len)
        tiles = c_rows.shape[0] // batch
        conv_state = c_rows.reshape(batch, tiles, SUBLANES, d_a)[:, -1, SUBLANES - (CONV_W - 1):, :]
        logf_b = logf.reshape(batch, seq_len, h_c)
        f_cum = _cumsum_time(jnp.transpose(logf_b, (0, 2, 1)))
        yc = _fox_prompt(zb, k, v, f_cum[:, :, :, None], f_cum[:, :, None, :], g_c,
                         batch=batch, seq_len=seq_len, n_heads=h_c, q_col_blk=5 * d_a // HEAD_DIM)
        kh, vh = k.reshape(batch, seq_len, h_c, HEAD_DIM), v.reshape(batch, seq_len, h_c, HEAD_DIM)
        logf_out = logf_b

    h = _oproj(ya, yb, yc, w_o)
    x1, x1b = _ln_res(x, h, ln1_g, ln1_b, alpha)

    if decode:
        hb, up_g, up_v = _ffn_up_decode(x1b, w_up, ffn_conv_w, ffn_conv_b, decode_ctx["ffn"])
        up_rows = jnp.concatenate([up_g, up_v], axis=-1)
        ffn_state = jnp.stack([decode_ctx["ffn"][:, 1, :], up_rows], axis=1)
    else:
        hb, up_g, up_v = _ffn_up(x1b, w_up, ffn_conv_w, ffn_conv_b, seq_len)
        tiles = up_g.shape[0] // batch
        up_rows = jnp.concatenate([up_g, up_v], axis=-1)
        ffn_state = up_rows.reshape(batch, tiles, SUBLANES, -1)[:, -1, SUBLANES - (CONV_W - 1):, :]
    ff = _matmul(hb, w_down, out_dtype=F32, name="down_proj")
    x2, x2b = _ln_res(x1, ff, ln2_g, ln2_b, alpha)
    return x2, x2b, (kh, vh, logf_out, conv_state, sgu_v, ffn_state)


def kernel(x_prompt, x_sample, cache_k, cache_v, cache_logf, page_table, state_conv, state_ffn_conv,
           w_in, b_f, sgu_ln_g, sgu_ln_b, sgu_w, sgu_b, conv_w, conv_b, mix_norm_g, w_o,
           ln1_g, ln1_b, w_up, ffn_conv_w, ffn_conv_b, w_down, ln2_g, ln2_b):
    depth, d, d_in = w_in.shape
    batch, seq_len, _ = x_prompt.shape
    dec_batch, dec_seq, _ = x_sample.shape
    assert dec_seq == 1, "the sample group is one new token per sequence"
    alpha = (2 * depth) ** 0.25
    h_c = b_f.shape[1]
    n_main = d_in - h_c

    b_f_p = jnp.pad(b_f, ((0, 0), (0, HEAD_DIM - h_c)))[:, None, :]
    sgu_b_t = jnp.transpose(sgu_b, (0, 2, 1))
    n_pool, page = cache_k.shape[1], cache_k.shape[2]
    cache_k2 = cache_k.reshape(depth, n_pool, page * h_c, HEAD_DIM)
    cache_v2 = cache_v.reshape(depth, n_pool, page * h_c, HEAD_DIM)
    cache_lf = cache_logf.reshape(depth, n_pool, 1, page * h_c)
    row = lambda a: a[:, None, :]

    xp = x_prompt.reshape(batch * seq_len, d)
    xs = x_sample.reshape(dec_batch, d)
    xpb, xsb = xp.astype(BF16), xs.astype(BF16)
    sp, ss = [], []
    for l in range(depth):
        w_in_b = w_in[l].astype(BF16)
        w_f = jnp.pad(w_in_b[:, n_main:], ((0, 0), (0, HEAD_DIM - h_c)))
        lw = (w_in_b, w_f, b_f_p[l], row(sgu_ln_g)[l], row(sgu_ln_b)[l], sgu_w[l], sgu_b_t[l],
              conv_w[l], row(conv_b)[l], row(mix_norm_g)[l], w_o[l].astype(BF16), row(ln1_g)[l], row(ln1_b)[l],
              w_up[l].astype(BF16), ffn_conv_w[l], row(ffn_conv_b)[l], w_down[l].astype(BF16),
              row(ln2_g)[l], row(ln2_b)[l])
        xp, xpb, st = _layer(xp, xpb, lw, alpha=alpha, batch=batch, seq_len=seq_len)
        sp.append(st)
        ctx = dict(conv=state_conv[l], ffn=state_ffn_conv[l], cache_k=cache_k2, cache_v=cache_v2,
                   cache_lf=cache_lf, page_table=page_table, layer=l)
        xs, xsb, st = _layer(xs, xsb, lw, alpha=alpha, batch=dec_batch, seq_len=1, decode_ctx=ctx)
        ss.append(st)

    def stack(sts, i):
        return jnp.stack([s[i] for s in sts])

    return (xp.reshape(batch, seq_len, d), xs.reshape(dec_batch, 1, d),
            stack(sp, 0), stack(sp, 1), stack(sp, 2), stack(sp, 3), stack(sp, 5), stack(sp, 4),
            stack(ss, 0), stack(ss, 1), stack(ss, 2), stack(ss, 3), stack(ss, 5), stack(ss, 4))
```

```python
import functools

import jax
import jax.numpy as jnp
from jax import lax
from jax.experimental import pallas as pl
from jax.experimental.pallas import tpu as pltpu

HEAD_DIM = 128
CHUNK = 128
CONV_W = 3
EPS = 1e-5
MASK_VALUE = -1e30
SUBLANES = 8
LANES = 128
STAT_ROWS = 128
VMEM_LIMIT_BYTES = 56 * 1024 * 1024
PAGES_PER_STEP = 8

F32 = jnp.float32
BF16 = jnp.bfloat16


def _pick(dim, candidates):
    for c in candidates:
        if dim % c == 0:
            return c
    return dim


def _params(*semantics):
    return pltpu.CompilerParams(dimension_semantics=semantics, vmem_limit_bytes=VMEM_LIMIT_BYTES)


def _mm_kernel(x_ref, w_ref, o_ref):
    o_ref[...] = jnp.dot(x_ref[...], w_ref[...], preferred_element_type=F32).astype(o_ref.dtype)


def _matmul(x, w, *, col_off=0, n_cols=None, out_dtype=F32, name="mm"):
    m, kdim = x.shape
    n_cols = w.shape[1] if n_cols is None else n_cols
    tm = _pick(m, (1024, 512, 256, 128))
    tn = next(c for c in (1024, 512, 256, 128) if n_cols % c == 0 and col_off % c == 0)
    joff = col_off // tn
    return pl.pallas_call(
        _mm_kernel,
        grid=(m // tm, n_cols // tn),
        in_specs=[pl.BlockSpec((tm, kdim), lambda i, j: (i, 0)),
                  pl.BlockSpec((kdim, tn), lambda i, j: (0, j + joff))],
        out_specs=pl.BlockSpec((tm, tn), lambda i, j: (i, j)),
        out_shape=jax.ShapeDtypeStruct((m, n_cols), out_dtype),
        compiler_params=_params("parallel", "parallel"),
        name=name,
    )(x, w)


def _cast_mm_kernel(x_ref, w_ref, z_ref, wb_ref, *, nk):
    wb = w_ref[...].astype(BF16)
    wb_ref[...] = wb
    part = jnp.dot(x_ref[...], wb, preferred_element_type=F32)
    if nk == 1:
        z_ref[...] = part
        return
    k = pl.program_id(1)

    @pl.when(k == 0)
    def _():
        z_ref[...] = part

    @pl.when(k > 0)
    def _():
        z_ref[...] += part


def _cast_matmul(x, w_all, layer, *, n_cols=None, name="cast_mm"):
    m, kdim = x.shape
    n_cols = w_all.shape[2] if n_cols is None else n_cols
    tk = kdim if kdim <= 4096 else _pick(kdim, (5504, 4096, 2048, 1024, 512, 256, 128))
    tn = _pick(n_cols, (512, 256, 128))
    nk = kdim // tk
    return pl.pallas_call(
        functools.partial(_cast_mm_kernel, nk=nk),
        grid=(n_cols // tn, nk),
        in_specs=[pl.BlockSpec((m, tk), lambda j, k: (0, k)),
                  pl.BlockSpec((None, tk, tn), lambda j, k: (layer, k, j))],
        out_specs=[pl.BlockSpec((m, tn), lambda j, k: (0, j)),
                   pl.BlockSpec((tk, tn), lambda j, k: (k, j))],
        out_shape=[jax.ShapeDtypeStruct((m, n_cols), F32), jax.ShapeDtypeStruct((kdim, n_cols), BF16)],
        compiler_params=_params("parallel", "arbitrary"),
        name=name,
    )(x, w_all)


def _logf_kernel(x_ref, w_ref, b_ref, o_ref):
    z = jnp.dot(x_ref[...], w_ref[...], preferred_element_type=F32) + b_ref[...]
    o_ref[...] = jnp.minimum(z, 0.0) - jnp.log1p(jnp.exp(-jnp.abs(z)))


def _logf(x, w_f, b_f):
    m, kdim = x.shape
    tm = _pick(m, (512, 256, 128))
    return pl.pallas_call(
        _logf_kernel,
        grid=(m // tm,),
        in_specs=[pl.BlockSpec((tm, kdim), lambda i: (i, 0)),
                  pl.BlockSpec((kdim, HEAD_DIM), lambda i: (0, 0)),
                  pl.BlockSpec((1, HEAD_DIM), lambda i: (0, 0))],
        out_specs=pl.BlockSpec((tm, HEAD_DIM), lambda i: (i, 0)),
        out_shape=jax.ShapeDtypeStruct((m, HEAD_DIM), F32),
        compiler_params=_params("parallel"),
        name="logf",
    )(x, w_f, b_f)


def _head_rms(y, g):
    return y * lax.rsqrt(jnp.mean(y * y, axis=-1, keepdims=True) + EPS) * g


def _shift_rows(c, prev):
    row = lax.broadcasted_iota(jnp.int32, c.shape, 0)
    p1 = prev[SUBLANES - 1:SUBLANES, :]
    p2 = prev[SUBLANES - 2:SUBLANES - 1, :]
    c1 = jnp.where(row == 0, p1, pltpu.roll(c, 1, axis=0))
    c2 = jnp.where(row == 0, p2, jnp.where(row == 1, p1, pltpu.roll(c, 2, axis=0)))
    return c1, c2


def _sgu_kernel(u_ref, v_ref, lng_ref, lnb_ref, w_ref, bst_ref, g_ref, y_ref, vln_ref, *, n_chunks, n_heads):
    u = jax.nn.gelu(u_ref[...].astype(F32))
    v = jax.nn.gelu(v_ref[...].astype(F32))
    mu = jnp.mean(v, axis=-1, keepdims=True)
    d = v - mu
    var = jnp.mean(d * d, axis=-1, keepdims=True)
    vln = d * lax.rsqrt(var + EPS) * lng_ref[...] + lnb_ref[...]
    vln_ref[...] = vln
    vb = vln.astype(BF16)
    t_idx = lax.broadcasted_iota(jnp.int32, (CHUNK, CHUNK), 0)
    s_idx = lax.broadcasted_iota(jnp.int32, (CHUNK, CHUNK), 1)
    causal = s_idx <= t_idx
    for h in range(n_heads):
        cols = slice(h * HEAD_DIM, (h + 1) * HEAD_DIM)
        wm = jnp.where(causal, w_ref[h], 0.0).astype(BF16)
        bias = bst_ref[:, h:h + 1]
        g = g_ref[:, cols]
        for c in range(n_chunks):
            rows = slice(c * CHUNK, (c + 1) * CHUNK)
            mixed = jnp.dot(wm, vb[rows, cols], preferred_element_type=F32) + bias
            y_ref[rows, cols] = _head_rms(u[rows, cols] * mixed, g).astype(y_ref.dtype)


def _sgu(zb, ln_g, ln_b, w_s, b_s_t, norm_g, d_a):
    m = zb.shape[0]
    n_heads = d_a // HEAD_DIM
    tm = _pick(m, (256, 128))
    return pl.pallas_call(
        functools.partial(_sgu_kernel, n_chunks=tm // CHUNK, n_heads=n_heads),
        grid=(m // tm,),
        in_specs=[pl.BlockSpec((tm, d_a), lambda i: (i, 0)),
                  pl.BlockSpec((tm, d_a), lambda i: (i, 1)),
                  pl.BlockSpec((1, d_a), lambda i: (0, 0)),
                  pl.BlockSpec((1, d_a), lambda i: (0, 0)),
                  pl.BlockSpec((n_heads, CHUNK, CHUNK), lambda i: (0, 0, 0)),
                  pl.BlockSpec((CHUNK, n_heads), lambda i: (0, 0)),
                  pl.BlockSpec((1, d_a), lambda i: (0, 0))],
        out_specs=[pl.BlockSpec((tm, d_a), lambda i: (i, 0)),
                   pl.BlockSpec((tm, d_a), lambda i: (i, 0))],
        out_shape=[jax.ShapeDtypeStruct((m, d_a), BF16), jax.ShapeDtypeStruct((m, d_a), F32)],
        compiler_params=_params("parallel"),
        name="sgu",
    )(zb, zb, ln_g, ln_b, w_s, b_s_t, norm_g)


def _bconv_kernel(*refs, tiles_per_seq, n_heads, decode):
    if decode:
        x_ref, gb_ref, gc_ref, cw_ref, cb_ref, g_ref, p2_ref, p1_ref, y_ref, st_ref = refs
    else:
        x_ref, gb_ref, gc_ref, cw_ref, cb_ref, g_ref, y_ref, st_ref, carry_ref = refs
    c = gc_ref[...].astype(F32) * x_ref[...].astype(F32)
    tm = c.shape[0]
    if decode:
        c1, c2 = p1_ref[...], p2_ref[...]
    else:
        start = pl.program_id(0) % tiles_per_seq == 0
        prev = jnp.where(start, 0.0, carry_ref[...])
        c1, c2 = _shift_rows(c, prev)
        carry_ref[...] = c[tm - SUBLANES:, :]
    st_ref[...] = c[tm - SUBLANES:, :]
    conv = cb_ref[...] + cw_ref[0:1, :] * c2 + cw_ref[1:2, :] * c1 + cw_ref[2:3, :] * c
    y = gb_ref[...].astype(F32) * conv
    for h in range(n_heads):
        cols = slice(h * HEAD_DIM, (h + 1) * HEAD_DIM)
        y_ref[:, cols] = _head_rms(y[:, cols], g_ref[:, cols]).astype(y_ref.dtype)


def _bconv(zb, conv_w, conv_b, norm_g, d_b, col_blk, seq_len, state=None):
    m = zb.shape[0]
    decode = state is not None
    tm = m if decode else _pick(seq_len, (512, 256, 128))
    n_tiles = m // tm
    row = lambda i: (i, 0)
    fixed = lambda i: (0, 0)
    in_specs = [pl.BlockSpec((tm, d_b), lambda i: (i, col_blk)),
                pl.BlockSpec((tm, d_b), lambda i: (i, col_blk + 1)),
                pl.BlockSpec((tm, d_b), lambda i: (i, col_blk + 2)),
                pl.BlockSpec((CONV_W, d_b), fixed),
                pl.BlockSpec((1, d_b), fixed),
                pl.BlockSpec((1, d_b), fixed)]
    args = [zb, zb, zb, conv_w, conv_b, norm_g]
    scratch = []
    if decode:
        in_specs += [pl.BlockSpec((tm, d_b), row), pl.BlockSpec((tm, d_b), row)]
        args += [state[:, 0, :], state[:, 1, :]]
    else:
        scratch = [pltpu.VMEM((SUBLANES, d_b), F32)]
    return pl.pallas_call(
        functools.partial(_bconv_kernel, tiles_per_seq=max(seq_len // tm, 1), n_heads=d_b // HEAD_DIM,
                          decode=decode),
        grid=(n_tiles,),
        in_specs=in_specs,
        out_specs=[pl.BlockSpec((tm, d_b), row),
                   pl.BlockSpec((None, SUBLANES, d_b), lambda i: (i, 0, 0))],
        out_shape=[jax.ShapeDtypeStruct((m, d_b), BF16),
                   jax.ShapeDtypeStruct((n_tiles, SUBLANES, d_b), F32)],
        scratch_shapes=scratch,
        compiler_params=_params("arbitrary"),
        name="bconv_decode" if decode else "bconv",
    )(*args)


def _cumsum_kernel(x_ref, o_ref):
    x = x_ref[...]
    t = x.shape[-1]
    lane = lax.broadcasted_iota(jnp.int32, x.shape, 1)
    shift = 1
    while shift < t:
        x = x + jnp.where(lane >= shift, pltpu.roll(x, shift, axis=1), 0.0)
        shift *= 2
    o_ref[...] = x


def _cumsum_time(logf_t):
    b, h, t = logf_t.shape
    return pl.pallas_call(
        _cumsum_kernel,
        grid=(b,),
        in_specs=[pl.BlockSpec((None, h, t), lambda i: (i, 0, 0))],
        out_specs=pl.BlockSpec((None, h, t), lambda i: (i, 0, 0)),
        out_shape=jax.ShapeDtypeStruct((b, h, t), F32),
        compiler_params=_params("parallel"),
        name="cumsum_logf",
    )(logf_t)


def _fox_prompt_kernel(q_ref, k_ref, v_ref, fq_ref, fk_ref, g_ref, o_ref, *, tq, nq):
    scale = HEAD_DIM ** -0.5
    kb = k_ref[...].astype(BF16)
    vb = v_ref[...].astype(BF16)
    causal = (lax.broadcasted_iota(jnp.int32, (tq, tq), 1) <= lax.broadcasted_iota(jnp.int32, (tq, tq), 0))
    contract_last = (((1,), (1,)), ((), ()))
    for qi in range(nq):
        rows = slice(qi * tq, (qi + 1) * tq)
        q = q_ref[rows, :]
        fq = fq_ref[rows, :]
        sd = lax.dot_general(q, kb[rows, :], contract_last, preferred_element_type=F32)
        sd = jnp.where(causal, sd * scale + fq - fk_ref[:, rows], MASK_VALUE)
        m = jnp.max(sd, axis=-1, keepdims=True)
        if qi > 0:
            past = slice(0, qi * tq)
            so = lax.dot_general(q, kb[past, :], contract_last, preferred_element_type=F32)
            so = so * scale + fq - fk_ref[:, past]
            m = jnp.maximum(m, jnp.max(so, axis=-1, keepdims=True))
            po = jnp.exp(so - m)
            l = jnp.sum(po, axis=-1, keepdims=True)
            acc = jnp.dot(po.astype(BF16), vb[past, :], preferred_element_type=F32)
        pd = jnp.exp(sd - m)
        pv = jnp.dot(pd.astype(BF16), vb[rows, :], preferred_element_type=F32)
        if qi > 0:
            l = l + jnp.sum(pd, axis=-1, keepdims=True)
            acc = acc + pv
        else:
            l = jnp.sum(pd, axis=-1, keepdims=True)
            acc = pv
        o_ref[rows, :] = _head_rms(acc / l, g_ref[...]).astype(o_ref.dtype)


def _fox_prompt(zb, k, v, f_col, f_row, norm_g, *, batch, seq_len, n_heads, q_col_blk):
    m = zb.shape[0]
    tq = _pick(seq_len, (512, 256, 128))
    seq_blk = lambda b, h: (b, h)
    return pl.pallas_call(
        functools.partial(_fox_prompt_kernel, tq=tq, nq=seq_len // tq),
        grid=(batch, n_heads),
        in_specs=[pl.BlockSpec((seq_len, HEAD_DIM), lambda b, h: (b, q_col_blk + h)),
                  pl.BlockSpec((seq_len, HEAD_DIM), seq_blk),
                  pl.BlockSpec((seq_len, HEAD_DIM), seq_blk),
                  pl.BlockSpec((None, None, seq_len, 1), lambda b, h: (b, h, 0, 0)),
                  pl.BlockSpec((None, None, 1, seq_len), lambda b, h: (b, h, 0, 0)),
                  pl.BlockSpec((1, HEAD_DIM), lambda b, h: (0, h))],
        out_specs=pl.BlockSpec((seq_len, HEAD_DIM), seq_blk),
        out_shape=jax.ShapeDtypeStruct((m, n_heads * HEAD_DIM), BF16),
        compiler_params=_params("parallel", "parallel"),
        name="fox_prompt",
    )(zb, k, v, f_col, f_row, norm_g)


def _fox_decode_kernel(pt_ref, q_ref, kn_ref, vn_ref, lfn_ref, g_ref, *rest, n_heads, page):
    del pt_ref
    pps = PAGES_PER_STEP
    k_refs, v_refs, lf_refs = rest[:pps], rest[pps:2 * pps], rest[2 * pps:3 * pps]
    o_ref, m_ref, l_ref, acc_ref, carry_ref = rest[3 * pps:]
    step = pl.program_id(1)
    scale = HEAD_DIM ** -0.5
    n_rows = page * n_heads

    @pl.when(step == 0)
    def _():
        m_ref[...] = jnp.full(m_ref.shape, MASK_VALUE, F32)
        l_ref[...] = jnp.zeros(l_ref.shape, F32)
        acc_ref[...] = jnp.zeros(acc_ref.shape, F32)
        carry_ref[...] = jnp.zeros(carry_ref.shape, F32)

    q = q_ref[...]
    lf = jnp.concatenate([r[...] for r in lf_refs], axis=0)
    lane = lax.broadcasted_iota(jnp.int32, lf.shape, 1)
    suffix, total = lf, lf
    shift = n_heads
    while shift < n_rows:
        suffix = suffix + jnp.where(lane < n_rows - shift, pltpu.roll(suffix, n_rows - shift, axis=1), 0.0)
        total = total + pltpu.roll(total, shift, axis=1)
        shift *= 2
    carry = carry_ref[...]
    lf_new = lfn_ref[...]
    bias = [None] * pps
    for r in reversed(range(pps)):
        bias[r] = suffix[r:r + 1, :] - lf[r:r + 1, :] + carry + lf_new
        carry = carry + total[r:r + 1, :]
    carry_ref[...] = carry

    own_head = ((lax.broadcasted_iota(jnp.int32, (n_heads, n_rows), 1) & (n_heads - 1))
                == lax.broadcasted_iota(jnp.int32, (n_heads, n_rows), 0))
    contract_last = (((1,), (1,)), ((), ()))
    s = [jnp.where(own_head,
                   lax.dot_general(q, k_refs[r][...].astype(BF16), contract_last,
                                   preferred_element_type=F32) * scale + bias[r],
                   MASK_VALUE) for r in range(pps)]
    m_prev = m_ref[...]
    m_new = m_prev
    for r in range(pps):
        m_new = jnp.maximum(m_new, jnp.max(s[r], axis=-1, keepdims=True))
    alpha = jnp.exp(m_prev - m_new)
    l_new = alpha * l_ref[...]
    acc = alpha * acc_ref[...]
    for r in range(pps):
        p = jnp.exp(s[r] - m_new)
        l_new = l_new + jnp.sum(p, axis=-1, keepdims=True)
        acc = acc + jnp.dot(p.astype(BF16), v_refs[r][...].astype(BF16), preferred_element_type=F32)
    l_ref[...] = l_new
    acc_ref[...] = acc
    m_ref[...] = m_new

    @pl.when(step == pl.num_programs(1) - 1)
    def _():
        kn = kn_ref[...].astype(BF16).astype(F32)
        vn = vn_ref[...].astype(BF16).astype(F32)
        s_self = jnp.sum(q.astype(F32) * kn, axis=-1, keepdims=True) * scale
        m_prev = m_ref[...]
        m_fin = jnp.maximum(m_prev, s_self)
        a = jnp.exp(m_prev - m_fin)
        p_self = jnp.exp(s_self - m_fin)
        l_fin = a * l_ref[...] + p_self
        o = (a * acc_ref[...] + p_self.astype(BF16).astype(F32) * vn) / l_fin
        o_ref[...] = _head_rms(o, g_ref[...]).astype(o_ref.dtype)


def _fox_decode(q, k_new, v_new, lf_new, norm_g, cache_k, cache_v, cache_lf, page_table, layer):
    bsz, n_heads, _ = q.shape
    n_rows = cache_k.shape[2]
    page = n_rows // n_heads
    n_pages = page_table.shape[1]
    pps = PAGES_PER_STEP
    assert n_pages % pps == 0 and n_heads & (n_heads - 1) == 0
    n_steps = n_pages // pps

    def page_idx(b, s, pt, r):
        return pt[b, (n_steps - 1 - s) * pps + r]

    per_b = lambda b, s, pt: (b, 0, 0)
    kv_spec = lambda r: pl.BlockSpec((None, None, n_rows, HEAD_DIM),
                                     lambda b, s, pt: (layer, page_idx(b, s, pt, r), 0, 0))
    lf_spec = lambda r: pl.BlockSpec((None, None, 1, n_rows),
                                     lambda b, s, pt: (layer, page_idx(b, s, pt, r), 0, 0))
    in_specs = ([pl.BlockSpec((None, n_heads, HEAD_DIM), per_b)] * 3
                + [pl.BlockSpec((None, 1, n_rows), per_b),
                   pl.BlockSpec((n_heads, HEAD_DIM), lambda b, s, pt: (0, 0))]
                + [kv_spec(r) for r in range(pps)] * 2
                + [lf_spec(r) for r in range(pps)])
    grid_spec = pltpu.PrefetchScalarGridSpec(
        num_scalar_prefetch=1,
        grid=(bsz, n_steps),
        in_specs=in_specs,
        out_specs=pl.BlockSpec((None, n_heads, HEAD_DIM), per_b),
        scratch_shapes=[pltpu.VMEM((n_heads, 1), F32), pltpu.VMEM((n_heads, 1), F32),
                        pltpu.VMEM((n_heads, HEAD_DIM), F32), pltpu.VMEM((1, n_rows), F32)])
    return pl.pallas_call(
        functools.partial(_fox_decode_kernel, n_heads=n_heads, page=page),
        grid_spec=grid_spec,
        out_shape=jax.ShapeDtypeStruct((bsz, n_heads, HEAD_DIM), BF16),
        compiler_params=_params("parallel", "arbitrary"),
        name="fox_decode",
    )(page_table, q, k_new, v_new, lf_new, norm_g,
      *([cache_k] * pps), *([cache_v] * pps), *([cache_lf] * pps))


def _proj_ln_kernel(*refs, n_lhs, nk, nj, alpha, tn, tn_out):
    lhs_refs, w_refs = refs[:n_lhs], refs[n_lhs:2 * n_lhs]
    res_ref, g_ref, b_ref, o_ref, ob_ref, acc_ref, mu_ref, rs_ref = refs[2 * n_lhs:]
    t = pl.program_id(1)
    n_mm = nk * nj
    cols = pl.ds(pl.multiple_of((t % nj) * tn, tn), tn)

    def product():
        out = jnp.dot(lhs_refs[0][...], w_refs[0][...], preferred_element_type=F32)
        for x_ref, w_ref in zip(lhs_refs[1:], w_refs[1:]):
            out += jnp.dot(x_ref[...], w_ref[...], preferred_element_type=F32)
        return out

    @pl.when(t < nj)
    def _():
        acc_ref[:, cols] = product() + alpha * res_ref[...]

    if nk > 1:
        @pl.when((t >= nj) & (t < n_mm))
        def _():
            acc_ref[:, cols] += product()

    @pl.when(t == n_mm)
    def _():
        def row_stats(r, carry):
            rows = pl.ds(pl.multiple_of(r * STAT_ROWS, STAT_ROWS), STAT_ROWS)
            y = acc_ref[rows, :]
            mu = jnp.mean(y, axis=-1, keepdims=True)
            d = y - mu
            mu_ref[rows, :] = mu
            rs_ref[rows, :] = lax.rsqrt(jnp.mean(d * d, axis=-1, keepdims=True) + EPS)
            return carry
        lax.fori_loop(0, acc_ref.shape[0] // STAT_ROWS, row_stats, 0)

    @pl.when(t >= n_mm)
    def _():
        ocols = pl.ds(pl.multiple_of((t - n_mm) * tn_out, tn_out), tn_out)
        o = (acc_ref[:, ocols] - mu_ref[...]) * rs_ref[...] * g_ref[...] + b_ref[...]
        o_ref[...] = o
        ob_ref[...] = o.astype(ob_ref.dtype)


def _proj_ln(lhs_list, w, w_row_blocks, resid, g, b, *, alpha, nk, tm, name):
    m, n = resid.shape
    tn = _pick(n, (512, 256, 128))
    tn_out = _pick(n, (2048, 1024, 512, 256, 128))
    nj, n_out = n // tn, n // tn_out
    n_mm = nk * nj
    n_lhs = len(lhs_list)
    widths = [x.shape[1] // nk for x in lhs_list]
    phase = lambda t: jnp.minimum(t, n_mm - 1) // nj
    col = lambda t: jnp.minimum(t, n_mm - 1) % nj
    out_col = lambda t: jnp.maximum(t - n_mm, 0)
    in_specs = [pl.BlockSpec((tm, kw), lambda i, t: (i, phase(t))) for kw in widths]
    in_specs += [pl.BlockSpec((kw, tn), lambda i, t, rb=rb: (rb + phase(t), col(t)))
                 for kw, rb in zip(widths, w_row_blocks)]
    in_specs += [pl.BlockSpec((tm, tn), lambda i, t: (i, jnp.minimum(t, nj - 1))),
                 pl.BlockSpec((1, tn_out), lambda i, t: (0, out_col(t))),
                 pl.BlockSpec((1, tn_out), lambda i, t: (0, out_col(t)))]
    out_spec = pl.BlockSpec((tm, tn_out), lambda i, t: (i, out_col(t)))
    return pl.pallas_call(
        functools.partial(_proj_ln_kernel, n_lhs=n_lhs, nk=nk, nj=nj, alpha=alpha, tn=tn, tn_out=tn_out),
        grid=(m // tm, n_mm + n_out),
        in_specs=in_specs,
        out_specs=[out_spec, out_spec],
        out_shape=[jax.ShapeDtypeStruct((m, n), F32), jax.ShapeDtypeStruct((m, n), BF16)],
        scratch_shapes=[pltpu.VMEM((tm, n), F32), pltpu.VMEM((tm, 1), F32), pltpu.VMEM((tm, 1), F32)],
        compiler_params=_params("parallel", "arbitrary"),
        name=name,
    )(*lhs_list, *([w] * n_lhs), resid, g, b)


def _ln_res_kernel(x_ref, h_ref, g_ref, b_ref, o_ref, ob_ref, *, alpha):
    y = alpha * x_ref[...] + h_ref[...]
    mu = jnp.mean(y, axis=-1, keepdims=True)
    d = y - mu
    var = jnp.mean(d * d, axis=-1, keepdims=True)
    o = d * lax.rsqrt(var + EPS) * g_ref[...] + b_ref[...]
    o_ref[...] = o
    ob_ref[...] = o.astype(ob_ref.dtype)


def _ln_res(x, h, g, b, alpha):
    m, d = x.shape
    tm = _pick(m, (256, 128))
    row = lambda i: (i, 0)
    fixed = lambda i: (0, 0)
    return pl.pallas_call(
        functools.partial(_ln_res_kernel, alpha=alpha),
        grid=(m // tm,),
        in_specs=[pl.BlockSpec((tm, d), row), pl.BlockSpec((tm, d), row),
                  pl.BlockSpec((1, d), fixed), pl.BlockSpec((1, d), fixed)],
        out_specs=[pl.BlockSpec((tm, d), row), pl.BlockSpec((tm, d), row)],
        out_shape=[jax.ShapeDtypeStruct((m, d), F32), jax.ShapeDtypeStruct((m, d), BF16)],
        compiler_params=_params("parallel"),
        name="ln_res",
    )(x, h, g, b)


def _conv_taps(c2, c1, c, cw_ref, cb_ref):
    return cb_ref[...] + cw_ref[0:1, :] * c2 + cw_ref[1:2, :] * c1 + cw_ref[2:3, :] * c


def _ffn_up_kernel(x_ref, wg_ref, wv_ref, cwg_ref, cwv_ref, cbg_ref, cbv_ref, h_ref, sg_ref, sv_ref,
                   raw_g_ref, raw_v_ref, carry_g_ref, carry_v_ref, *, tiles_per_seq, nf, n_tiles):
    s = pl.program_id(0)
    tm = x_ref.shape[0]

    @pl.when(s == 0)
    def _():
        raw_g_ref[...] = jnp.zeros(raw_g_ref.shape, F32)
        raw_v_ref[...] = jnp.zeros(raw_v_ref.shape, F32)

    def taps(raw_ref, cw_ref, cb_ref):
        return _conv_taps(raw_ref[pl.ds(SUBLANES - 2, tm), :], raw_ref[pl.ds(SUBLANES - 1, tm), :],
                          raw_ref[pl.ds(SUBLANES, tm), :], cw_ref, cb_ref)

    gate = taps(raw_g_ref, cwg_ref, cbg_ref)
    val = taps(raw_v_ref, cwv_ref, cbv_ref)
    h_ref[...] = (gate * jax.nn.sigmoid(gate) * val).astype(h_ref.dtype)
    sg_ref[...] = raw_g_ref[pl.ds(tm, SUBLANES), :]
    sv_ref[...] = raw_v_ref[pl.ds(tm, SUBLANES), :]

    sc = jnp.minimum(s, n_tiles * nf - 1)
    j = sc % nf
    start = (sc // nf) % tiles_per_seq == 0
    x = x_ref[...]
    for w_ref, raw_ref, carry_ref in ((wg_ref, raw_g_ref, carry_g_ref), (wv_ref, raw_v_ref, carry_v_ref)):
        u = jnp.dot(x, w_ref[...], preferred_element_type=F32)
        raw_ref[pl.ds(0, SUBLANES), :] = jnp.where(start, 0.0, carry_ref[j])
        raw_ref[pl.ds(SUBLANES, tm), :] = u
        carry_ref[j] = u[tm - SUBLANES:, :]


def _ffn_up(xb, w_gate, w_val, conv_w, conv_b, seq_len):
    m, d = xb.shape
    d_ff = w_gate.shape[1]
    tf = _pick(d_ff, (256, 128))
    nf = d_ff // tf
    tm = _pick(seq_len, (1024, 512, 256, 128))
    n_tiles = m // tm
    last = n_tiles * nf - 1
    cur = lambda s: jnp.minimum(s, last)
    prev = lambda s: jnp.maximum(s - 1, 0)
    return pl.pallas_call(
        functools.partial(_ffn_up_kernel, tiles_per_seq=seq_len // tm, nf=nf, n_tiles=n_tiles),
        grid=(n_tiles * nf + 1,),
        in_specs=[pl.BlockSpec((tm, d), lambda s: (cur(s) // nf, 0)),
                  pl.BlockSpec((d, tf), lambda s: (0, cur(s) % nf)),
                  pl.BlockSpec((d, tf), lambda s: (0, cur(s) % nf)),
                  pl.BlockSpec((CONV_W, tf), lambda s: (0, prev(s) % nf)),
                  pl.BlockSpec((CONV_W, tf), lambda s: (0, prev(s) % nf + nf)),
                  pl.BlockSpec((1, tf), lambda s: (0, prev(s) % nf)),
                  pl.BlockSpec((1, tf), lambda s: (0, prev(s) % nf + nf))],
        out_specs=[pl.BlockSpec((tm, tf), lambda s: (prev(s) // nf, prev(s) % nf)),
                   pl.BlockSpec((None, SUBLANES, tf), lambda s: (prev(s) // nf, 0, prev(s) % nf)),
                   pl.BlockSpec((None, SUBLANES, tf), lambda s: (prev(s) // nf, 0, prev(s) % nf))],
        out_shape=[jax.ShapeDtypeStruct((m, d_ff), BF16),
                   jax.ShapeDtypeStruct((n_tiles, SUBLANES, d_ff), F32),
                   jax.ShapeDtypeStruct((n_tiles, SUBLANES, d_ff), F32)],
        scratch_shapes=[pltpu.VMEM((tm + SUBLANES, tf), F32), pltpu.VMEM((tm + SUBLANES, tf), F32),
                        pltpu.VMEM((nf, SUBLANES, tf), F32), pltpu.VMEM((nf, SUBLANES, tf), F32)],
        compiler_params=_params("arbitrary"),
        name="ffn_up",
    )(xb, w_gate, w_val, conv_w, conv_w, conv_b, conv_b)


def _ffn_up_decode_kernel(x_ref, wg_ref, wv_ref, cwg_ref, cwv_ref, cbg_ref, cbv_ref,
                          pg2_ref, pg1_ref, pv2_ref, pv1_ref, h_ref, ug_ref, uv_ref, wgb_ref, wvb_ref):
    x = x_ref[...]
    wg = wg_ref[...].astype(BF16)
    wv = wv_ref[...].astype(BF16)
    wgb_ref[...] = wg
    wvb_ref[...] = wv
    ug = jnp.dot(x, wg, preferred_element_type=F32)
    uv = jnp.dot(x, wv, preferred_element_type=F32)
    ug_ref[...] = ug
    uv_ref[...] = uv
    gate = _conv_taps(pg2_ref[...], pg1_ref[...], ug, cwg_ref, cbg_ref)
    val = _conv_taps(pv2_ref[...], pv1_ref[...], uv, cwv_ref, cbv_ref)
    h_ref[...] = (gate * jax.nn.sigmoid(gate) * val).astype(h_ref.dtype)


def _ffn_up_decode(xb, w_up_all, layer, conv_w, conv_b, state):
    m, d = xb.shape
    d_ff = w_up_all.shape[2] // 2
    tf = _pick(d_ff, (256, 128))
    nf = d_ff // tf
    gcol = lambda j: (0, j)
    vcol = lambda j: (0, j + nf)
    s0, s1 = state[:, 0, :], state[:, 1, :]
    return pl.pallas_call(
        _ffn_up_decode_kernel,
        grid=(nf,),
        in_specs=[pl.BlockSpec((m, d), lambda j: (0, 0)),
                  pl.BlockSpec((None, d, tf), lambda j: (layer, 0, j)),
                  pl.BlockSpec((None, d, tf), lambda j: (layer, 0, j + nf)),
                  pl.BlockSpec((CONV_W, tf), gcol), pl.BlockSpec((CONV_W, tf), vcol),
                  pl.BlockSpec((1, tf), gcol), pl.BlockSpec((1, tf), vcol),
                  pl.BlockSpec((m, tf), gcol), pl.BlockSpec((m, tf), gcol),
                  pl.BlockSpec((m, tf), vcol), pl.BlockSpec((m, tf), vcol)],
        out_specs=[pl.BlockSpec((m, tf), gcol), pl.BlockSpec((m, tf), gcol), pl.BlockSpec((m, tf), gcol),
                   pl.BlockSpec((d, tf), gcol), pl.BlockSpec((d, tf), gcol)],
        out_shape=[jax.ShapeDtypeStruct((m, d_ff), BF16),
                   jax.ShapeDtypeStruct((m, d_ff), F32), jax.ShapeDtypeStruct((m, d_ff), F32),
                   jax.ShapeDtypeStruct((d, d_ff), BF16), jax.ShapeDtypeStruct((d, d_ff), BF16)],
        compiler_params=_params("parallel"),
        name="ffn_up_decode",
    )(xb, w_up_all, w_up_all, conv_w, conv_w, conv_b, conv_b, s0, s1, s0, s1)


def _layer(x, xb, lw, mats, *, alpha, batch, seq_len, decode_ctx=None):
    (w_f, b_f, sgu_ln_g, sgu_ln_b, sgu_w, sgu_b_t, conv_w, conv_b, mix_norm_g, ln1_g, ln1_b,
     ffn_conv_w, ffn_conv_b, ln2_g, ln2_b) = lw
    m, d = x.shape
    n_heads = d // HEAD_DIM
    d_a = (n_heads // 4) * HEAD_DIM
    d_c = d - 2 * d_a
    h_c = d_c // HEAD_DIM
    assert d_c == 2 * d_a
    decode = decode_ctx is not None

    n_zb = 2 * d_a + 3 * d_a + d_c
    if decode:
        layer = decode_ctx["layer"]
        w_in_all, w_o_all, w_up_all, w_down_all = mats
        z, w_in = _cast_matmul(xb, w_in_all, layer, n_cols=n_zb + 2 * d_c, name="in_proj_decode")
        zb, k, v = z[:, :n_zb].astype(BF16), z[:, n_zb:n_zb + d_c], z[:, n_zb + d_c:]
    else:
        w_in, w_o, w_gate, w_val, w_down = mats
        zb = _matmul(xb, w_in, col_off=0, n_cols=n_zb, out_dtype=BF16, name="in_proj_mix")
        k = _matmul(xb, w_in, col_off=n_zb, n_cols=d_c, out_dtype=F32, name="in_proj_k")
        v = _matmul(xb, w_in, col_off=n_zb + d_c, n_cols=d_c, out_dtype=F32, name="in_proj_v")
    logf = _logf(xb, w_f, b_f)[:, :h_c]

    g_a, g_b, g_c = mix_norm_g[:, :d_a], mix_norm_g[:, d_a:2 * d_a], mix_norm_g[:, 2 * d_a:]

    if decode:
        zuv = jnp.zeros((m, CHUNK, 2 * d_a), BF16).at[:, 0, :].set(zb[:, :2 * d_a]).reshape(m * CHUNK, 2 * d_a)
        ya_full, vln_full = _sgu(zuv, sgu_ln_g, sgu_ln_b, sgu_w, sgu_b_t, g_a, d_a)
        ya = ya_full.reshape(m, CHUNK, d_a)[:, 0, :]
        sgu_v = vln_full.reshape(m, CHUNK, d_a)[:, :1, :]
        yb, c_rows = _bconv(zb, conv_w, conv_b, g_b, d_a, 2, 1, state=decode_ctx["conv"])
        conv_state = jnp.stack([decode_ctx["conv"][:, 1, :], c_rows[0]], axis=1)
        q3 = zb[:, 5 * d_a:].reshape(m, h_c, HEAD_DIM)
        page = decode_ctx["cache_k"].shape[2] // h_c
        lf_rows = jnp.tile(logf, (1, page))[:, None, :]
        yc = _fox_decode(q3, k.reshape(m, h_c, HEAD_DIM), v.reshape(m, h_c, HEAD_DIM), lf_rows,
                         g_c.reshape(h_c, HEAD_DIM), decode_ctx["cache_k"], decode_ctx["cache_v"],
                         decode_ctx["cache_lf"], decode_ctx["page_table"], decode_ctx["layer"]).reshape(m, d_c)
        kh, vh = k.reshape(m, 1, h_c, HEAD_DIM), v.reshape(m, 1, h_c, HEAD_DIM)
        logf_out = logf.reshape(m, 1, h_c)
    else:
        ya, vln = _sgu(zb, sgu_ln_g, sgu_ln_b, sgu_w, sgu_b_t, g_a, d_a)
        last_chunk = ((seq_len - 1) // CHUNK) * CHUNK
        sgu_v = vln.reshape(batch, seq_len, d_a)[:, last_chunk:, :]
        yb, c_rows = _bconv(zb, conv_w, conv_b, g_b, d_a, 2, seq_len)
        tiles = c_rows.shape[0] // batch
        conv_state = c_rows.reshape(batch, tiles, SUBLANES, d_a)[:, -1, SUBLANES - (CONV_W - 1):, :]
        logf_b = logf.reshape(batch, seq_len, h_c)
        f_cum = _cumsum_time(jnp.transpose(logf_b, (0, 2, 1)))
        yc = _fox_prompt(zb, k, v, f_cum[:, :, :, None], f_cum[:, :, None, :], g_c,
                         batch=batch, seq_len=seq_len, n_heads=h_c, q_col_blk=5 * d_a // HEAD_DIM)
        kh, vh = k.reshape(batch, seq_len, h_c, HEAD_DIM), v.reshape(batch, seq_len, h_c, HEAD_DIM)
        logf_out = logf_b

    if decode:
        h, w_o = _cast_matmul(jnp.concatenate([ya, yb, yc], axis=-1), w_o_all, layer, name="oproj_decode")
        x1, x1b = _ln_res(x, h, ln1_g, ln1_b, alpha)
        hb, up_g, up_v, w_gate, w_val = _ffn_up_decode(x1b, w_up_all, layer, ffn_conv_w, ffn_conv_b,
                                                       decode_ctx["ffn"])
        up_rows = jnp.concatenate([up_g, up_v], axis=-1)
        ffn_state = jnp.stack([decode_ctx["ffn"][:, 1, :], up_rows], axis=1)
        ff, w_down = _cast_matmul(hb, w_down_all, layer, name="down_proj_decode")
        x2, x2b = _ln_res(x1, ff, ln2_g, ln2_b, alpha)
        made = (w_in, w_o, w_gate, w_val, w_down)
    else:
        x1, x1b = _proj_ln([ya, yb, yc], w_o, [0, 1, 1], x, ln1_g, ln1_b, alpha=alpha, nk=1,
                           tm=_pick(m, (512, 256, 128)), name="oproj_ln")
        hb, up_g, up_v = _ffn_up(x1b, w_gate, w_val, ffn_conv_w, ffn_conv_b, seq_len)
        tiles = up_g.shape[0] // batch
        up_rows = jnp.concatenate([up_g, up_v], axis=-1)
        ffn_state = up_rows.reshape(batch, tiles, SUBLANES, -1)[:, -1, SUBLANES - (CONV_W - 1):, :]
        d_ff = hb.shape[1]
        nk = 2 if (d_ff // 2) % LANES == 0 else 1
        x2, x2b = _proj_ln([hb], w_down, [0], x1, ln2_g, ln2_b, alpha=alpha, nk=nk,
                           tm=_pick(m, (512, 256, 128)), name="down_ln")
        made = None
    return x2, x2b, (kh, vh, logf_out, conv_state, sgu_v, ffn_state), made


def kernel(x_prompt, x_sample, cache_k, cache_v, cache_logf, page_table, state_conv, state_ffn_conv,
           w_in, b_f, sgu_ln_g, sgu_ln_b, sgu_w, sgu_b, conv_w, conv_b, mix_norm_g, w_o,
           ln1_g, ln1_b, w_up, ffn_conv_w, ffn_conv_b, w_down, ln2_g, ln2_b):
    depth, d, d_in = w_in.shape
    batch, seq_len, _ = x_prompt.shape
    dec_batch, dec_seq, _ = x_sample.shape
    assert dec_seq == 1, "the sample group is one new token per sequence"
    alpha = (2 * depth) ** 0.25
    h_c = b_f.shape[1]
    n_main = d_in - h_c

    w_f = jnp.pad(w_in[:, :, n_main:], ((0, 0), (0, 0), (0, HEAD_DIM - h_c))).astype(BF16)
    b_f_p = jnp.pad(b_f, ((0, 0), (0, HEAD_DIM - h_c)))[:, None, :]
    sgu_b_t = jnp.transpose(sgu_b, (0, 2, 1))
    n_pool, page = cache_k.shape[1], cache_k.shape[2]
    cache_k2 = cache_k.reshape(depth, n_pool, page * h_c, HEAD_DIM)
    cache_v2 = cache_v.reshape(depth, n_pool, page * h_c, HEAD_DIM)
    cache_lf = cache_logf.reshape(depth, n_pool, 1, page * h_c)
    row = lambda a: a[:, None, :]

    xp = x_prompt.reshape(batch * seq_len, d)
    xs = x_sample.reshape(dec_batch, d)
    xpb, xsb = xp.astype(BF16), xs.astype(BF16)
    sp, ss = [], []
    for l in range(depth):
        lw = (w_f[l], b_f_p[l], row(sgu_ln_g)[l], row(sgu_ln_b)[l], sgu_w[l], sgu_b_t[l],
              conv_w[l], row(conv_b)[l], row(mix_norm_g)[l], row(ln1_g)[l], row(ln1_b)[l],
              ffn_conv_w[l], row(ffn_conv_b)[l], row(ln2_g)[l], row(ln2_b)[l])
        ctx = dict(conv=state_conv[l], ffn=state_ffn_conv[l], cache_k=cache_k2, cache_v=cache_v2,
                   cache_lf=cache_lf, page_table=page_table, layer=l)
        xs, xsb, st, mats = _layer(xs, xsb, lw, (w_in, w_o, w_up, w_down), alpha=alpha, batch=dec_batch,
                                   seq_len=1, decode_ctx=ctx)
        ss.append(st)
        xp, xpb, st, _ = _layer(xp, xpb, lw, mats, alpha=alpha, batch=batch, seq_len=seq_len)
        sp.append(st)

    def stack(sts, i):
        return jnp.stack([s[i] for s in sts])

    return (xp.reshape(batch, seq_len, d), xs.reshape(dec_batch, 1, d),
            stack(sp, 0), stack(sp, 1), stack(sp, 2), stack(sp, 3), stack(sp, 5), stack(sp, 4),
            stack(ss, 0), stack(ss, 1), stack(ss, 2), stack(ss, 3), stack(ss, 5), stack(ss, 4))
```

```python
import functools

import jax
import jax.numpy as jnp
from jax import lax
from jax.experimental import pallas as pl
from jax.experimental.pallas import tpu as pltpu

HEAD_DIM = 128
CHUNK = 128
CONV_W = 3
EPS = 1e-5
MASK_VALUE = -1e30
LOG2_E = 1.4426950408889634
SUBLANES = 8
LANES = 128
STAT_ROWS = 128
VMEM_LIMIT_BYTES = 56 * 1024 * 1024
PAGES_PER_STEP = 8

F32 = jnp.float32
BF16 = jnp.bfloat16


def _pick(dim, candidates):
    for c in candidates:
        if dim % c == 0:
            return c
    return dim


def _params(*semantics):
    return pltpu.CompilerParams(dimension_semantics=semantics, vmem_limit_bytes=VMEM_LIMIT_BYTES)


CONTRACT_LAST = (((1,), (1,)), ((), ()))


def _mm_nk_kernel(x_ref, w_ref, o_ref):
    o_ref[...] = lax.dot_general(x_ref[...], w_ref[...], CONTRACT_LAST,
                                 preferred_element_type=F32).astype(o_ref.dtype)


def _matmul_nk(x, w, *, col_off=0, n_cols=None, out_dtype=F32, name="mm"):
    m, kdim = x.shape
    n_cols = w.shape[0] if n_cols is None else n_cols
    tm = _pick(m, (1024, 512, 256, 128))
    tn = next(c for c in (1024, 512, 256, 128) if n_cols % c == 0 and col_off % c == 0)
    joff = col_off // tn
    return pl.pallas_call(
        _mm_nk_kernel,
        grid=(m // tm, n_cols // tn),
        in_specs=[pl.BlockSpec((tm, kdim), lambda i, j: (i, 0)),
                  pl.BlockSpec((tn, kdim), lambda i, j: (j + joff, 0))],
        out_specs=pl.BlockSpec((tm, tn), lambda i, j: (i, j)),
        out_shape=jax.ShapeDtypeStruct((m, n_cols), out_dtype),
        compiler_params=_params("parallel", "parallel"),
        name=name,
    )(x, w)


def _cast_mm_kernel(x_ref, w_ref, z_ref, wb_ref, *, nk, w_is_nk):
    wb = w_ref[...].astype(BF16)
    wb_ref[...] = wb
    if w_is_nk:
        part = lax.dot_general(x_ref[...], wb, CONTRACT_LAST, preferred_element_type=F32)
    else:
        part = jnp.dot(x_ref[...], wb, preferred_element_type=F32)
    if nk == 1:
        z_ref[...] = part
        return
    k = pl.program_id(1)

    @pl.when(k == 0)
    def _():
        z_ref[...] = part

    @pl.when(k > 0)
    def _():
        z_ref[...] += part


def _cast_matmul(x, w_all, layer, *, n_cols=None, w_is_nk=False, name="cast_mm"):
    m, kdim = x.shape
    n_cols = w_all.shape[1 if w_is_nk else 2] if n_cols is None else n_cols
    tk = kdim if kdim <= 4096 else _pick(kdim, (5504, 4096, 2048, 1024, 512, 256, 128))
    tn = _pick(n_cols, (512, 256, 128))
    nk = kdim // tk
    if w_is_nk:
        w_spec = pl.BlockSpec((None, tn, tk), lambda j, k: (layer, j, k))
        wb_spec = pl.BlockSpec((tn, tk), lambda j, k: (j, k))
        wb_shape = (n_cols, kdim)
    else:
        w_spec = pl.BlockSpec((None, tk, tn), lambda j, k: (layer, k, j))
        wb_spec = pl.BlockSpec((tk, tn), lambda j, k: (k, j))
        wb_shape = (kdim, n_cols)
    return pl.pallas_call(
        functools.partial(_cast_mm_kernel, nk=nk, w_is_nk=w_is_nk),
        grid=(n_cols // tn, nk),
        in_specs=[pl.BlockSpec((m, tk), lambda j, k: (0, k)), w_spec],
        out_specs=[pl.BlockSpec((m, tn), lambda j, k: (0, j)), wb_spec],
        out_shape=[jax.ShapeDtypeStruct((m, n_cols), F32), jax.ShapeDtypeStruct(wb_shape, BF16)],
        compiler_params=_params("parallel", "arbitrary"),
        name=name,
    )(x, w_all)


def _logf_kernel(x_ref, w_ref, b_ref, o_ref):
    z = jnp.dot(x_ref[...], w_ref[...], preferred_element_type=F32) + b_ref[...]
    o_ref[...] = jnp.minimum(z, 0.0) - jnp.log1p(jnp.exp(-jnp.abs(z)))


def _logf(x, w_f, b_f):
    m, kdim = x.shape
    tm = _pick(m, (512, 256, 128))
    return pl.pallas_call(
        _logf_kernel,
        grid=(m // tm,),
        in_specs=[pl.BlockSpec((tm, kdim), lambda i: (i, 0)),
                  pl.BlockSpec((kdim, HEAD_DIM), lambda i: (0, 0)),
                  pl.BlockSpec((1, HEAD_DIM), lambda i: (0, 0))],
        out_specs=pl.BlockSpec((tm, HEAD_DIM), lambda i: (i, 0)),
        out_shape=jax.ShapeDtypeStruct((m, HEAD_DIM), F32),
        compiler_params=_params("parallel"),
        name="logf",
    )(x, w_f, b_f)


def _head_rms(y, g):
    return y * lax.rsqrt(jnp.mean(y * y, axis=-1, keepdims=True) + EPS) * g


def _shift_rows(c, prev):
    row = lax.broadcasted_iota(jnp.int32, c.shape, 0)
    p1 = prev[SUBLANES - 1:SUBLANES, :]
    p2 = prev[SUBLANES - 2:SUBLANES - 1, :]
    c1 = jnp.where(row == 0, p1, pltpu.roll(c, 1, axis=0))
    c2 = jnp.where(row == 0, p2, jnp.where(row == 1, p1, pltpu.roll(c, 2, axis=0)))
    return c1, c2


def _sgu_kernel(u_ref, v_ref, lng_ref, lnb_ref, w_ref, bst_ref, g_ref, y_ref, vln_ref, *, n_chunks, n_heads):
    u = jax.nn.gelu(u_ref[...].astype(F32))
    v = jax.nn.gelu(v_ref[...].astype(F32))
    mu = jnp.mean(v, axis=-1, keepdims=True)
    d = v - mu
    var = jnp.mean(d * d, axis=-1, keepdims=True)
    vln = d * lax.rsqrt(var + EPS) * lng_ref[...] + lnb_ref[...]
    vln_ref[...] = vln
    vb = vln.astype(BF16)
    t_idx = lax.broadcasted_iota(jnp.int32, (CHUNK, CHUNK), 0)
    s_idx = lax.broadcasted_iota(jnp.int32, (CHUNK, CHUNK), 1)
    causal = s_idx <= t_idx
    for h in range(n_heads):
        cols = slice(h * HEAD_DIM, (h + 1) * HEAD_DIM)
        wm = jnp.where(causal, w_ref[h], 0.0).astype(BF16)
        bias = bst_ref[:, h:h + 1]
        g = g_ref[:, cols]
        for c in range(n_chunks):
            rows = slice(c * CHUNK, (c + 1) * CHUNK)
            mixed = jnp.dot(wm, vb[rows, cols], preferred_element_type=F32) + bias
            y_ref[rows, cols] = _head_rms(u[rows, cols] * mixed, g).astype(y_ref.dtype)


def _sgu(zb, ln_g, ln_b, w_s, b_s_t, norm_g, d_a):
    m = zb.shape[0]
    n_heads = d_a // HEAD_DIM
    tm = _pick(m, (256, 128))
    return pl.pallas_call(
        functools.partial(_sgu_kernel, n_chunks=tm // CHUNK, n_heads=n_heads),
        grid=(m // tm,),
        in_specs=[pl.BlockSpec((tm, d_a), lambda i: (i, 0)),
                  pl.BlockSpec((tm, d_a), lambda i: (i, 1)),
                  pl.BlockSpec((1, d_a), lambda i: (0, 0)),
                  pl.BlockSpec((1, d_a), lambda i: (0, 0)),
                  pl.BlockSpec((n_heads, CHUNK, CHUNK), lambda i: (0, 0, 0)),
                  pl.BlockSpec((CHUNK, n_heads), lambda i: (0, 0)),
                  pl.BlockSpec((1, d_a), lambda i: (0, 0))],
        out_specs=[pl.BlockSpec((tm, d_a), lambda i: (i, 0)),
                   pl.BlockSpec((tm, d_a), lambda i: (i, 0))],
        out_shape=[jax.ShapeDtypeStruct((m, d_a), BF16), jax.ShapeDtypeStruct((m, d_a), F32)],
        compiler_params=_params("parallel"),
        name="sgu",
    )(zb, zb, ln_g, ln_b, w_s, b_s_t, norm_g)


def _bconv_kernel(*refs, tiles_per_seq, n_heads, decode):
    if decode:
        x_ref, gb_ref, gc_ref, cw_ref, cb_ref, g_ref, p2_ref, p1_ref, y_ref, st_ref = refs
    else:
        x_ref, gb_ref, gc_ref, cw_ref, cb_ref, g_ref, y_ref, st_ref, carry_ref = refs
    c = gc_ref[...].astype(F32) * x_ref[...].astype(F32)
    tm = c.shape[0]
    if decode:
        c1, c2 = p1_ref[...], p2_ref[...]
    else:
        start = pl.program_id(0) % tiles_per_seq == 0
        prev = jnp.where(start, 0.0, carry_ref[...])
        c1, c2 = _shift_rows(c, prev)
        carry_ref[...] = c[tm - SUBLANES:, :]
    st_ref[...] = c[tm - SUBLANES:, :]
    conv = cb_ref[...] + cw_ref[0:1, :] * c2 + cw_ref[1:2, :] * c1 + cw_ref[2:3, :] * c
    y = gb_ref[...].astype(F32) * conv
    for h in range(n_heads):
        cols = slice(h * HEAD_DIM, (h + 1) * HEAD_DIM)
        y_ref[:, cols] = _head_rms(y[:, cols], g_ref[:, cols]).astype(y_ref.dtype)


def _bconv(zb, conv_w, conv_b, norm_g, d_b, col_blk, seq_len, state=None):
    m = zb.shape[0]
    decode = state is not None
    tm = m if decode else _pick(seq_len, (512, 256, 128))
    n_tiles = m // tm
    row = lambda i: (i, 0)
    fixed = lambda i: (0, 0)
    in_specs = [pl.BlockSpec((tm, d_b), lambda i: (i, col_blk)),
                pl.BlockSpec((tm, d_b), lambda i: (i, col_blk + 1)),
                pl.BlockSpec((tm, d_b), lambda i: (i, col_blk + 2)),
                pl.BlockSpec((CONV_W, d_b), fixed),
                pl.BlockSpec((1, d_b), fixed),
                pl.BlockSpec((1, d_b), fixed)]
    args = [zb, zb, zb, conv_w, conv_b, norm_g]
    scratch = []
    if decode:
        in_specs += [pl.BlockSpec((tm, d_b), row), pl.BlockSpec((tm, d_b), row)]
        args += [state[:, 0, :], state[:, 1, :]]
    else:
        scratch = [pltpu.VMEM((SUBLANES, d_b), F32)]
    return pl.pallas_call(
        functools.partial(_bconv_kernel, tiles_per_seq=max(seq_len // tm, 1), n_heads=d_b // HEAD_DIM,
                          decode=decode),
        grid=(n_tiles,),
        in_specs=in_specs,
        out_specs=[pl.BlockSpec((tm, d_b), row),
                   pl.BlockSpec((None, SUBLANES, d_b), lambda i: (i, 0, 0))],
        out_shape=[jax.ShapeDtypeStruct((m, d_b), BF16),
                   jax.ShapeDtypeStruct((n_tiles, SUBLANES, d_b), F32)],
        scratch_shapes=scratch,
        compiler_params=_params("arbitrary"),
        name="bconv_decode" if decode else "bconv",
    )(*args)


def _cumsum_kernel(x_ref, o_ref):
    x = x_ref[...]
    t = x.shape[-1]
    lane = lax.broadcasted_iota(jnp.int32, x.shape, 1)
    shift = 1
    while shift < t:
        x = x + jnp.where(lane >= shift, pltpu.roll(x, shift, axis=1), 0.0)
        shift *= 2
    o_ref[...] = x


def _cumsum_time(logf_t):
    b, h, t = logf_t.shape
    return pl.pallas_call(
        _cumsum_kernel,
        grid=(b,),
        in_specs=[pl.BlockSpec((None, h, t), lambda i: (i, 0, 0))],
        out_specs=pl.BlockSpec((None, h, t), lambda i: (i, 0, 0)),
        out_shape=jax.ShapeDtypeStruct((b, h, t), F32),
        compiler_params=_params("parallel"),
        name="cumsum_logf",
    )(logf_t)


def _fox_prompt_kernel(q_ref, k_ref, v_ref, fq_ref, fk_ref, g_ref, o_ref, *, tq, nq):
    scale = HEAD_DIM ** -0.5 * LOG2_E
    kb = k_ref[...].astype(BF16)
    vb = v_ref[...].astype(BF16)
    fk_all = fk_ref[...] * LOG2_E
    causal = (lax.broadcasted_iota(jnp.int32, (tq, tq), 1) <= lax.broadcasted_iota(jnp.int32, (tq, tq), 0))
    for qi in range(nq):
        rows = slice(qi * tq, (qi + 1) * tq)
        q = q_ref[rows, :]
        fq = fq_ref[rows, :] * LOG2_E
        sd = lax.dot_general(q, kb[rows, :], CONTRACT_LAST, preferred_element_type=F32)
        sd = jnp.where(causal, sd * scale + fq - fk_all[:, rows], MASK_VALUE)
        m = jnp.max(sd, axis=-1, keepdims=True)
        if qi > 0:
            past = slice(0, qi * tq)
            so = lax.dot_general(q, kb[past, :], CONTRACT_LAST, preferred_element_type=F32)
            so = so * scale + fq - fk_all[:, past]
            m = jnp.maximum(m, jnp.max(so, axis=-1, keepdims=True))
            po = jnp.exp2(so - m)
            l = jnp.sum(po, axis=-1, keepdims=True)
            acc = jnp.dot(po.astype(BF16), vb[past, :], preferred_element_type=F32)
        pd = jnp.exp2(sd - m)
        pv = jnp.dot(pd.astype(BF16), vb[rows, :], preferred_element_type=F32)
        if qi > 0:
            l = l + jnp.sum(pd, axis=-1, keepdims=True)
            acc = acc + pv
        else:
            l = jnp.sum(pd, axis=-1, keepdims=True)
            acc = pv
        o_ref[rows, :] = _head_rms(acc / l, g_ref[...]).astype(o_ref.dtype)


def _fox_prompt(zb, k, v, f_col, f_row, norm_g, *, batch, seq_len, n_heads, q_col_blk):
    m = zb.shape[0]
    tq = _pick(seq_len, (512, 256, 128))
    seq_blk = lambda b, h: (b, h)
    return pl.pallas_call(
        functools.partial(_fox_prompt_kernel, tq=tq, nq=seq_len // tq),
        grid=(batch, n_heads),
        in_specs=[pl.BlockSpec((seq_len, HEAD_DIM), lambda b, h: (b, q_col_blk + h)),
                  pl.BlockSpec((seq_len, HEAD_DIM), seq_blk),
                  pl.BlockSpec((seq_len, HEAD_DIM), seq_blk),
                  pl.BlockSpec((None, None, seq_len, 1), lambda b, h: (b, h, 0, 0)),
                  pl.BlockSpec((None, None, 1, seq_len), lambda b, h: (b, h, 0, 0)),
                  pl.BlockSpec((1, HEAD_DIM), lambda b, h: (0, h))],
        out_specs=pl.BlockSpec((seq_len, HEAD_DIM), seq_blk),
        out_shape=jax.ShapeDtypeStruct((m, n_heads * HEAD_DIM), BF16),
        compiler_params=_params("parallel", "parallel"),
        name="fox_prompt",
    )(zb, k, v, f_col, f_row, norm_g)


def _fox_decode_kernel(pt_ref, q_ref, kn_ref, vn_ref, lfn_ref, g_ref, *rest, n_heads, page):
    del pt_ref
    pps = PAGES_PER_STEP
    k_refs, v_refs, lf_refs = rest[:pps], rest[pps:2 * pps], rest[2 * pps:3 * pps]
    o_ref, m_ref, l_ref, acc_ref, carry_ref = rest[3 * pps:]
    step = pl.program_id(1)
    scale = HEAD_DIM ** -0.5
    n_rows = page * n_heads

    @pl.when(step == 0)
    def _():
        m_ref[...] = jnp.full(m_ref.shape, MASK_VALUE, F32)
        l_ref[...] = jnp.zeros(l_ref.shape, F32)
        acc_ref[...] = jnp.zeros(acc_ref.shape, F32)
        carry_ref[...] = jnp.zeros(carry_ref.shape, F32)

    q = q_ref[...]
    lf = jnp.concatenate([r[...] for r in lf_refs], axis=0)
    lane = lax.broadcasted_iota(jnp.int32, lf.shape, 1)
    suffix, total = lf, lf
    shift = n_heads
    while shift < n_rows:
        suffix = suffix + jnp.where(lane < n_rows - shift, pltpu.roll(suffix, n_rows - shift, axis=1), 0.0)
        total = total + pltpu.roll(total, shift, axis=1)
        shift *= 2
    carry = carry_ref[...]
    lf_new = lfn_ref[...]
    bias = [None] * pps
    for r in reversed(range(pps)):
        bias[r] = suffix[r:r + 1, :] - lf[r:r + 1, :] + carry + lf_new
        carry = carry + total[r:r + 1, :]
    carry_ref[...] = carry

    own_head = ((lax.broadcasted_iota(jnp.int32, (n_heads, n_rows), 1) & (n_heads - 1))
                == lax.broadcasted_iota(jnp.int32, (n_heads, n_rows), 0))
    contract_last = (((1,), (1,)), ((), ()))
    s = [jnp.where(own_head,
                   lax.dot_general(q, k_refs[r][...].astype(BF16), contract_last,
                                   preferred_element_type=F32) * scale + bias[r],
                   MASK_VALUE) for r in range(pps)]
    m_prev = m_ref[...]
    m_new = m_prev
    for r in range(pps):
        m_new = jnp.maximum(m_new, jnp.max(s[r], axis=-1, keepdims=True))
    alpha = jnp.exp(m_prev - m_new)
    l_new = alpha * l_ref[...]
    acc = alpha * acc_ref[...]
    for r in range(pps):
        p = jnp.exp(s[r] - m_new)
        l_new = l_new + jnp.sum(p, axis=-1, keepdims=True)
        acc = acc + jnp.dot(p.astype(BF16), v_refs[r][...].astype(BF16), preferred_element_type=F32)
    l_ref[...] = l_new
    acc_ref[...] = acc
    m_ref[...] = m_new

    @pl.when(step == pl.num_programs(1) - 1)
    def _():
        kn = kn_ref[...].astype(BF16).astype(F32)
        vn = vn_ref[...].astype(BF16).astype(F32)
        s_self = jnp.sum(q.astype(F32) * kn, axis=-1, keepdims=True) * scale
        m_prev = m_ref[...]
        m_fin = jnp.maximum(m_prev, s_self)
        a = jnp.exp(m_prev - m_fin)
        p_self = jnp.exp(s_self - m_fin)
        l_fin = a * l_ref[...] + p_self
        o = (a * acc_ref[...] + p_self.astype(BF16).astype(F32) * vn) / l_fin
        o_ref[...] = _head_rms(o, g_ref[...]).astype(o_ref.dtype)


def _fox_decode(q, k_new, v_new, lf_new, norm_g, cache_k, cache_v, cache_lf, page_table, layer):
    bsz, n_heads, _ = q.shape
    n_rows = cache_k.shape[2]
    page = n_rows // n_heads
    n_pages = page_table.shape[1]
    pps = PAGES_PER_STEP
    assert n_pages % pps == 0 and n_heads & (n_heads - 1) == 0
    n_steps = n_pages // pps

    def page_idx(b, s, pt, r):
        return pt[b, (n_steps - 1 - s) * pps + r]

    per_b = lambda b, s, pt: (b, 0, 0)
    kv_spec = lambda r: pl.BlockSpec((None, None, n_rows, HEAD_DIM),
                                     lambda b, s, pt: (layer, page_idx(b, s, pt, r), 0, 0))
    lf_spec = lambda r: pl.BlockSpec((None, None, 1, n_rows),
                                     lambda b, s, pt: (layer, page_idx(b, s, pt, r), 0, 0))
    in_specs = ([pl.BlockSpec((None, n_heads, HEAD_DIM), per_b)] * 3
                + [pl.BlockSpec((None, 1, n_rows), per_b),
                   pl.BlockSpec((n_heads, HEAD_DIM), lambda b, s, pt: (0, 0))]
                + [kv_spec(r) for r in range(pps)] * 2
                + [lf_spec(r) for r in range(pps)])
    grid_spec = pltpu.PrefetchScalarGridSpec(
        num_scalar_prefetch=1,
        grid=(bsz, n_steps),
        in_specs=in_specs,
        out_specs=pl.BlockSpec((None, n_heads, HEAD_DIM), per_b),
        scratch_shapes=[pltpu.VMEM((n_heads, 1), F32), pltpu.VMEM((n_heads, 1), F32),
                        pltpu.VMEM((n_heads, HEAD_DIM), F32), pltpu.VMEM((1, n_rows), F32)])
    return pl.pallas_call(
        functools.partial(_fox_decode_kernel, n_heads=n_heads, page=page),
        grid_spec=grid_spec,
        out_shape=jax.ShapeDtypeStruct((bsz, n_heads, HEAD_DIM), BF16),
        compiler_params=_params("parallel", "arbitrary"),
        name="fox_decode",
    )(page_table, q, k_new, v_new, lf_new, norm_g,
      *([cache_k] * pps), *([cache_v] * pps), *([cache_lf] * pps))


def _proj_ln_kernel(*refs, n_lhs, nk, nj, alpha, tn, tn_out):
    lhs_refs, w_refs = refs[:n_lhs], refs[n_lhs:2 * n_lhs]
    res_ref, g_ref, b_ref, o_ref, ob_ref, acc_ref, mu_ref, rs_ref = refs[2 * n_lhs:]
    t = pl.program_id(1)
    n_mm = nk * nj
    cols = pl.ds(pl.multiple_of((t % nj) * tn, tn), tn)

    def product():
        out = jnp.dot(lhs_refs[0][...], w_refs[0][...], preferred_element_type=F32)
        for x_ref, w_ref in zip(lhs_refs[1:], w_refs[1:]):
            out += jnp.dot(x_ref[...], w_ref[...], preferred_element_type=F32)
        return out

    @pl.when(t < nj)
    def _():
        acc_ref[:, cols] = product() + alpha * res_ref[...]

    if nk > 1:
        @pl.when((t >= nj) & (t < n_mm))
        def _():
            acc_ref[:, cols] += product()

    @pl.when(t == n_mm)
    def _():
        def row_stats(r, carry):
            rows = pl.ds(pl.multiple_of(r * STAT_ROWS, STAT_ROWS), STAT_ROWS)
            y = acc_ref[rows, :]
            mu = jnp.mean(y, axis=-1, keepdims=True)
            d = y - mu
            mu_ref[rows, :] = mu
            rs_ref[rows, :] = lax.rsqrt(jnp.mean(d * d, axis=-1, keepdims=True) + EPS)
            return carry
        lax.fori_loop(0, acc_ref.shape[0] // STAT_ROWS, row_stats, 0)

    @pl.when(t >= n_mm)
    def _():
        ocols = pl.ds(pl.multiple_of((t - n_mm) * tn_out, tn_out), tn_out)
        o = (acc_ref[:, ocols] - mu_ref[...]) * rs_ref[...] * g_ref[...] + b_ref[...]
        o_ref[...] = o
        ob_ref[...] = o.astype(ob_ref.dtype)


def _proj_ln(lhs_list, w, w_row_blocks, resid, g, b, *, alpha, nk, tm, tn_out, name):
    m, n = resid.shape
    tn = _pick(n, (512, 256, 128))
    tn_out = _pick(n, (tn_out, 512, 256, 128))
    nj, n_out = n // tn, n // tn_out
    n_mm = nk * nj
    n_lhs = len(lhs_list)
    widths = [x.shape[1] // nk for x in lhs_list]
    phase = lambda t: jnp.minimum(t, n_mm - 1) // nj
    col = lambda t: jnp.minimum(t, n_mm - 1) % nj
    out_col = lambda t: jnp.maximum(t - n_mm, 0)
    in_specs = [pl.BlockSpec((tm, kw), lambda i, t: (i, phase(t))) for kw in widths]
    in_specs += [pl.BlockSpec((kw, tn), lambda i, t, rb=rb: (rb + phase(t), col(t)))
                 for kw, rb in zip(widths, w_row_blocks)]
    in_specs += [pl.BlockSpec((tm, tn), lambda i, t: (i, jnp.minimum(t, nj - 1))),
                 pl.BlockSpec((1, tn_out), lambda i, t: (0, out_col(t))),
                 pl.BlockSpec((1, tn_out), lambda i, t: (0, out_col(t)))]
    out_spec = pl.BlockSpec((tm, tn_out), lambda i, t: (i, out_col(t)))
    return pl.pallas_call(
        functools.partial(_proj_ln_kernel, n_lhs=n_lhs, nk=nk, nj=nj, alpha=alpha, tn=tn, tn_out=tn_out),
        grid=(m // tm, n_mm + n_out),
        in_specs=in_specs,
        out_specs=[out_spec, out_spec],
        out_shape=[jax.ShapeDtypeStruct((m, n), F32), jax.ShapeDtypeStruct((m, n), BF16)],
        scratch_shapes=[pltpu.VMEM((tm, n), F32), pltpu.VMEM((tm, 1), F32), pltpu.VMEM((tm, 1), F32)],
        compiler_params=_params("parallel", "arbitrary"),
        name=name,
    )(*lhs_list, *([w] * n_lhs), resid, g, b)


def _ln_res_kernel(x_ref, h_ref, g_ref, b_ref, o_ref, ob_ref, *, alpha):
    y = alpha * x_ref[...] + h_ref[...]
    mu = jnp.mean(y, axis=-1, keepdims=True)
    d = y - mu
    var = jnp.mean(d * d, axis=-1, keepdims=True)
    o = d * lax.rsqrt(var + EPS) * g_ref[...] + b_ref[...]
    o_ref[...] = o
    ob_ref[...] = o.astype(ob_ref.dtype)


def _ln_res(x, h, g, b, alpha):
    m, d = x.shape
    tm = _pick(m, (256, 128))
    row = lambda i: (i, 0)
    fixed = lambda i: (0, 0)
    return pl.pallas_call(
        functools.partial(_ln_res_kernel, alpha=alpha),
        grid=(m // tm,),
        in_specs=[pl.BlockSpec((tm, d), row), pl.BlockSpec((tm, d), row),
                  pl.BlockSpec((1, d), fixed), pl.BlockSpec((1, d), fixed)],
        out_specs=[pl.BlockSpec((tm, d), row), pl.BlockSpec((tm, d), row)],
        out_shape=[jax.ShapeDtypeStruct((m, d), F32), jax.ShapeDtypeStruct((m, d), BF16)],
        compiler_params=_params("parallel"),
        name="ln_res",
    )(x, h, g, b)


def _conv_taps(c2, c1, c, cw_ref, cb_ref):
    return cb_ref[...] + cw_ref[0:1, :] * c2 + cw_ref[1:2, :] * c1 + cw_ref[2:3, :] * c


def _ffn_up_kernel(x_ref, wg_ref, wv_ref, cwg_ref, cwv_ref, cbg_ref, cbv_ref, h_ref, sg_ref, sv_ref,
                   raw_g_ref, raw_v_ref, carry_g_ref, carry_v_ref, *, tiles_per_seq, nf, n_tiles):
    s = pl.program_id(0)
    tm = x_ref.shape[0]

    @pl.when(s == 0)
    def _():
        raw_g_ref[...] = jnp.zeros(raw_g_ref.shape, F32)
        raw_v_ref[...] = jnp.zeros(raw_v_ref.shape, F32)

    def taps(raw_ref, cw_ref, cb_ref):
        return _conv_taps(raw_ref[pl.ds(SUBLANES - 2, tm), :], raw_ref[pl.ds(SUBLANES - 1, tm), :],
                          raw_ref[pl.ds(SUBLANES, tm), :], cw_ref, cb_ref)

    gate = taps(raw_g_ref, cwg_ref, cbg_ref)
    val = taps(raw_v_ref, cwv_ref, cbv_ref)
    h_ref[...] = (gate * jax.nn.sigmoid(gate) * val).astype(h_ref.dtype)
    sg_ref[...] = raw_g_ref[pl.ds(tm, SUBLANES), :]
    sv_ref[...] = raw_v_ref[pl.ds(tm, SUBLANES), :]

    sc = jnp.minimum(s, n_tiles * nf - 1)
    j = sc % nf
    start = (sc // nf) % tiles_per_seq == 0
    x = x_ref[...]
    for w_ref, raw_ref, carry_ref in ((wg_ref, raw_g_ref, carry_g_ref), (wv_ref, raw_v_ref, carry_v_ref)):
        u = jnp.dot(x, w_ref[...], preferred_element_type=F32)
        raw_ref[pl.ds(0, SUBLANES), :] = jnp.where(start, 0.0, carry_ref[j])
        raw_ref[pl.ds(SUBLANES, tm), :] = u
        carry_ref[j] = u[tm - SUBLANES:, :]


def _ffn_up(xb, w_gate, w_val, conv_w, conv_b, seq_len):
    m, d = xb.shape
    d_ff = w_gate.shape[1]
    tf = _pick(d_ff, (256, 128))
    nf = d_ff // tf
    tm = _pick(seq_len, (1024, 512, 256, 128))
    n_tiles = m // tm
    last = n_tiles * nf - 1
    cur = lambda s: jnp.minimum(s, last)
    prev = lambda s: jnp.maximum(s - 1, 0)
    return pl.pallas_call(
        functools.partial(_ffn_up_kernel, tiles_per_seq=seq_len // tm, nf=nf, n_tiles=n_tiles),
        grid=(n_tiles * nf + 1,),
        in_specs=[pl.BlockSpec((tm, d), lambda s: (cur(s) // nf, 0)),
                  pl.BlockSpec((d, tf), lambda s: (0, cur(s) % nf)),
                  pl.BlockSpec((d, tf), lambda s: (0, cur(s) % nf)),
                  pl.BlockSpec((CONV_W, tf), lambda s: (0, prev(s) % nf)),
                  pl.BlockSpec((CONV_W, tf), lambda s: (0, prev(s) % nf + nf)),
                  pl.BlockSpec((1, tf), lambda s: (0, prev(s) % nf)),
                  pl.BlockSpec((1, tf), lambda s: (0, prev(s) % nf + nf))],
        out_specs=[pl.BlockSpec((tm, tf), lambda s: (prev(s) // nf, prev(s) % nf)),
                   pl.BlockSpec((None, SUBLANES, tf), lambda s: (prev(s) // nf, 0, prev(s) % nf)),
                   pl.BlockSpec((None, SUBLANES, tf), lambda s: (prev(s) // nf, 0, prev(s) % nf))],
        out_shape=[jax.ShapeDtypeStruct((m, d_ff), BF16),
                   jax.ShapeDtypeStruct((n_tiles, SUBLANES, d_ff), F32),
                   jax.ShapeDtypeStruct((n_tiles, SUBLANES, d_ff), F32)],
        scratch_shapes=[pltpu.VMEM((tm + SUBLANES, tf), F32), pltpu.VMEM((tm + SUBLANES, tf), F32),
                        pltpu.VMEM((nf, SUBLANES, tf), F32), pltpu.VMEM((nf, SUBLANES, tf), F32)],
        compiler_params=_params("arbitrary"),
        name="ffn_up",
    )(xb, w_gate, w_val, conv_w, conv_w, conv_b, conv_b)


def _ffn_up_decode_kernel(x_ref, wg_ref, wv_ref, cwg_ref, cwv_ref, cbg_ref, cbv_ref,
                          pg2_ref, pg1_ref, pv2_ref, pv1_ref, h_ref, ug_ref, uv_ref, wgb_ref, wvb_ref):
    x = x_ref[...]
    wg = wg_ref[...].astype(BF16)
    wv = wv_ref[...].astype(BF16)
    wgb_ref[...] = wg
    wvb_ref[...] = wv
    ug = jnp.dot(x, wg, preferred_element_type=F32)
    uv = jnp.dot(x, wv, preferred_element_type=F32)
    ug_ref[...] = ug
    uv_ref[...] = uv
    gate = _conv_taps(pg2_ref[...], pg1_ref[...], ug, cwg_ref, cbg_ref)
    val = _conv_taps(pv2_ref[...], pv1_ref[...], uv, cwv_ref, cbv_ref)
    h_ref[...] = (gate * jax.nn.sigmoid(gate) * val).astype(h_ref.dtype)


def _ffn_up_decode(xb, w_up_all, layer, conv_w, conv_b, state):
    m, d = xb.shape
    d_ff = w_up_all.shape[2] // 2
    tf = _pick(d_ff, (256, 128))
    nf = d_ff // tf
    gcol = lambda j: (0, j)
    vcol = lambda j: (0, j + nf)
    s0, s1 = state[:, 0, :], state[:, 1, :]
    return pl.pallas_call(
        _ffn_up_decode_kernel,
        grid=(nf,),
        in_specs=[pl.BlockSpec((m, d), lambda j: (0, 0)),
                  pl.BlockSpec((None, d, tf), lambda j: (layer, 0, j)),
                  pl.BlockSpec((None, d, tf), lambda j: (layer, 0, j + nf)),
                  pl.BlockSpec((CONV_W, tf), gcol), pl.BlockSpec((CONV_W, tf), vcol),
                  pl.BlockSpec((1, tf), gcol), pl.BlockSpec((1, tf), vcol),
                  pl.BlockSpec((m, tf), gcol), pl.BlockSpec((m, tf), gcol),
                  pl.BlockSpec((m, tf), vcol), pl.BlockSpec((m, tf), vcol)],
        out_specs=[pl.BlockSpec((m, tf), gcol), pl.BlockSpec((m, tf), gcol), pl.BlockSpec((m, tf), gcol),
                   pl.BlockSpec((d, tf), gcol), pl.BlockSpec((d, tf), gcol)],
        out_shape=[jax.ShapeDtypeStruct((m, d_ff), BF16),
                   jax.ShapeDtypeStruct((m, d_ff), F32), jax.ShapeDtypeStruct((m, d_ff), F32),
                   jax.ShapeDtypeStruct((d, d_ff), BF16), jax.ShapeDtypeStruct((d, d_ff), BF16)],
        compiler_params=_params("parallel"),
        name="ffn_up_decode",
    )(xb, w_up_all, w_up_all, conv_w, conv_w, conv_b, conv_b, s0, s1, s0, s1)


def _layer(x, xb, lw, mats, *, alpha, batch, seq_len, decode_ctx=None):
    (w_f, b_f, sgu_ln_g, sgu_ln_b, sgu_w, sgu_b_t, conv_w, conv_b, mix_norm_g, ln1_g, ln1_b,
     ffn_conv_w, ffn_conv_b, ln2_g, ln2_b) = lw
    m, d = x.shape
    n_heads = d // HEAD_DIM
    d_a = (n_heads // 4) * HEAD_DIM
    d_c = d - 2 * d_a
    h_c = d_c // HEAD_DIM
    assert d_c == 2 * d_a
    decode = decode_ctx is not None

    n_zb = 2 * d_a + 3 * d_a + d_c
    if decode:
        layer = decode_ctx["layer"]
        w_in_all, w_o_all, w_up_all, w_down_all = mats
        z, w_in = _cast_matmul(xb, w_in_all, layer, n_cols=n_zb + 2 * d_c, w_is_nk=True, name="in_proj_decode")
        zb, k, v = z[:, :n_zb].astype(BF16), z[:, n_zb:n_zb + d_c], z[:, n_zb + d_c:]
    else:
        w_in, w_o, w_gate, w_val, w_down = mats
        zb = _matmul_nk(xb, w_in, col_off=0, n_cols=n_zb, out_dtype=BF16, name="in_proj_mix")
        k = _matmul_nk(xb, w_in, col_off=n_zb, n_cols=d_c, out_dtype=F32, name="in_proj_k")
        v = _matmul_nk(xb, w_in, col_off=n_zb + d_c, n_cols=d_c, out_dtype=F32, name="in_proj_v")
    logf = _logf(xb, w_f, b_f)[:, :h_c]

    g_a, g_b, g_c = mix_norm_g[:, :d_a], mix_norm_g[:, d_a:2 * d_a], mix_norm_g[:, 2 * d_a:]

    if decode:
        zuv = jnp.zeros((m, CHUNK, 2 * d_a), BF16).at[:, 0, :].set(zb[:, :2 * d_a]).reshape(m * CHUNK, 2 * d_a)
        ya_full, vln_full = _sgu(zuv, sgu_ln_g, sgu_ln_b, sgu_w, sgu_b_t, g_a, d_a)
        ya = ya_full.reshape(m, CHUNK, d_a)[:, 0, :]
        sgu_v = vln_full.reshape(m, CHUNK, d_a)[:, :1, :]
        yb, c_rows = _bconv(zb, conv_w, conv_b, g_b, d_a, 2, 1, state=decode_ctx["conv"])
        conv_state = jnp.stack([decode_ctx["conv"][:, 1, :], c_rows[0]], axis=1)
        q3 = zb[:, 5 * d_a:].reshape(m, h_c, HEAD_DIM)
        page = decode_ctx["cache_k"].shape[2] // h_c
        lf_rows = jnp.tile(logf, (1, page))[:, None, :]
        yc = _fox_decode(q3, k.reshape(m, h_c, HEAD_DIM), v.reshape(m, h_c, HEAD_DIM), lf_rows,
                         g_c.reshape(h_c, HEAD_DIM), decode_ctx["cache_k"], decode_ctx["cache_v"],
                         decode_ctx["cache_lf"], decode_ctx["page_table"], decode_ctx["layer"]).reshape(m, d_c)
        kh, vh = k.reshape(m, 1, h_c, HEAD_DIM), v.reshape(m, 1, h_c, HEAD_DIM)
        logf_out = logf.reshape(m, 1, h_c)
    else:
        ya, vln = _sgu(zb, sgu_ln_g, sgu_ln_b, sgu_w, sgu_b_t, g_a, d_a)
        last_chunk = ((seq_len - 1) // CHUNK) * CHUNK
        sgu_v = vln.reshape(batch, seq_len, d_a)[:, last_chunk:, :]
        yb, c_rows = _bconv(zb, conv_w, conv_b, g_b, d_a, 2, seq_len)
        tiles = c_rows.shape[0] // batch
        conv_state = c_rows.reshape(batch, tiles, SUBLANES, d_a)[:, -1, SUBLANES - (CONV_W - 1):, :]
        logf_b = logf.reshape(batch, seq_len, h_c)
        f_cum = _cumsum_time(jnp.transpose(logf_b, (0, 2, 1)))
        yc = _fox_prompt(zb, k, v, f_cum[:, :, :, None], f_cum[:, :, None, :], g_c,
                         batch=batch, seq_len=seq_len, n_heads=h_c, q_col_blk=5 * d_a // HEAD_DIM)
        kh, vh = k.reshape(batch, seq_len, h_c, HEAD_DIM), v.reshape(batch, seq_len, h_c, HEAD_DIM)
        logf_out = logf_b

    if decode:
        h, w_o = _cast_matmul(jnp.concatenate([ya, yb, yc], axis=-1), w_o_all, layer, name="oproj_decode")
        x1, x1b = _ln_res(x, h, ln1_g, ln1_b, alpha)
        hb, up_g, up_v, w_gate, w_val = _ffn_up_decode(x1b, w_up_all, layer, ffn_conv_w, ffn_conv_b,
                                                       decode_ctx["ffn"])
        up_rows = jnp.concatenate([up_g, up_v], axis=-1)
        ffn_state = jnp.stack([decode_ctx["ffn"][:, 1, :], up_rows], axis=1)
        ff, w_down = _cast_matmul(hb, w_down_all, layer, name="down_proj_decode")
        x2, x2b = _ln_res(x1, ff, ln2_g, ln2_b, alpha)
        made = (w_in, w_o, w_gate, w_val, w_down)
    else:
        x1, x1b = _proj_ln([ya, yb, yc], w_o, [0, 1, 1], x, ln1_g, ln1_b, alpha=alpha, nk=1,
                           tm=_pick(m, (1024, 512, 256, 128)), tn_out=512, name="oproj_ln")
        hb, up_g, up_v = _ffn_up(x1b, w_gate, w_val, ffn_conv_w, ffn_conv_b, seq_len)
        tiles = up_g.shape[0] // batch
        up_rows = jnp.concatenate([up_g, up_v], axis=-1)
        ffn_state = up_rows.reshape(batch, tiles, SUBLANES, -1)[:, -1, SUBLANES - (CONV_W - 1):, :]
        d_ff = hb.shape[1]
        nk = 2 if (d_ff // 2) % LANES == 0 else 1
        x2, x2b = _proj_ln([hb], w_down, [0], x1, ln2_g, ln2_b, alpha=alpha, nk=nk,
                           tm=_pick(m, (512, 256, 128)), tn_out=2048, name="down_ln")
        made = None
    return x2, x2b, (kh, vh, logf_out, conv_state, sgu_v, ffn_state), made


def kernel(x_prompt, x_sample, cache_k, cache_v, cache_logf, page_table, state_conv, state_ffn_conv,
           w_in, b_f, sgu_ln_g, sgu_ln_b, sgu_w, sgu_b, conv_w, conv_b, mix_norm_g, w_o,
           ln1_g, ln1_b, w_up, ffn_conv_w, ffn_conv_b, w_down, ln2_g, ln2_b):
    depth, d, d_in = w_in.shape
    batch, seq_len, _ = x_prompt.shape
    dec_batch, dec_seq, _ = x_sample.shape
    assert dec_seq == 1, "the sample group is one new token per sequence"
    alpha = (2 * depth) ** 0.25
    h_c = b_f.shape[1]
    n_main = d_in - h_c

    w_in_nk = jnp.swapaxes(w_in, 1, 2)
    w_f = jnp.pad(w_in[:, :, n_main:], ((0, 0), (0, 0), (0, HEAD_DIM - h_c))).astype(BF16)
    b_f_p = jnp.pad(b_f, ((0, 0), (0, HEAD_DIM - h_c)))[:, None, :]
    sgu_b_t = jnp.transpose(sgu_b, (0, 2, 1))
    n_pool, page = cache_k.shape[1], cache_k.shape[2]
    cache_k2 = cache_k.reshape(depth, n_pool, page * h_c, HEAD_DIM)
    cache_v2 = cache_v.reshape(depth, n_pool, page * h_c, HEAD_DIM)
    cache_lf = cache_logf.reshape(depth, n_pool, 1, page * h_c)
    row = lambda a: a[:, None, :]

    xp = x_prompt.reshape(batch * seq_len, d)
    xs = x_sample.reshape(dec_batch, d)
    xpb, xsb = xp.astype(BF16), xs.astype(BF16)
    sp, ss = [], []
    for l in range(depth):
        lw = (w_f[l], b_f_p[l], row(sgu_ln_g)[l], row(sgu_ln_b)[l], sgu_w[l], sgu_b_t[l],
              conv_w[l], row(conv_b)[l], row(mix_norm_g)[l], row(ln1_g)[l], row(ln1_b)[l],
              ffn_conv_w[l], row(ffn_conv_b)[l], row(ln2_g)[l], row(ln2_b)[l])
        ctx = dict(conv=state_conv[l], ffn=state_ffn_conv[l], cache_k=cache_k2, cache_v=cache_v2,
                   cache_lf=cache_lf, page_table=page_table, layer=l)
        xs, xsb, st, mats = _layer(xs, xsb, lw, (w_in_nk, w_o, w_up, w_down), alpha=alpha, batch=dec_batch,
                                   seq_len=1, decode_ctx=ctx)
        ss.append(st)
        xp, xpb, st, _ = _layer(xp, xpb, lw, mats, alpha=alpha, batch=batch, seq_len=seq_len)
        sp.append(st)

    def stack(sts, i):
        return jnp.stack([s[i] for s in sts])

    return (xp.reshape(batch, seq_len, d), xs.reshape(dec_batch, 1, d),
            stack(sp, 0), stack(sp, 1), stack(sp, 2), stack(sp, 3), stack(sp, 5), stack(sp, 4),
            stack(ss, 0), stack(ss, 1), stack(ss, 2), stack(ss, 3), stack(ss, 5), stack(ss, 4))
```

```python
import functools

import jax
import jax.numpy as jnp
from jax import lax
from jax.experimental import pallas as pl
from jax.experimental.pallas import tpu as pltpu

HEAD_DIM = 128
CHUNK = 128
CONV_W = 3
EPS = 1e-5
MASK_VALUE = -1e30
LOG2_E = 1.4426950408889634
SUBLANES = 8
LANES = 128
STAT_ROWS = 128
VMEM_LIMIT_BYTES = 56 * 1024 * 1024
PAGES_PER_STEP = 8

F32 = jnp.float32
BF16 = jnp.bfloat16


def _pick(dim, candidates):
    for c in candidates:
        if dim % c == 0:
            return c
    return dim


def _params(*semantics):
    return pltpu.CompilerParams(dimension_semantics=semantics, vmem_limit_bytes=VMEM_LIMIT_BYTES)


CONTRACT_LAST = (((1,), (1,)), ((), ()))


def _mm_nk_kernel(x_ref, w_ref, o_ref):
    o_ref[...] = lax.dot_general(x_ref[...], w_ref[...], CONTRACT_LAST,
                                 preferred_element_type=F32).astype(o_ref.dtype)


def _matmul_nk(x, w, *, col_off=0, n_cols=None, out_dtype=F32, name="mm"):
    m, kdim = x.shape
    n_cols = w.shape[0] if n_cols is None else n_cols
    tm = _pick(m, (1024, 512, 256, 128))
    tn = next(c for c in (1024, 512, 256, 128) if n_cols % c == 0 and col_off % c == 0)
    joff = col_off // tn
    return pl.pallas_call(
        _mm_nk_kernel,
        grid=(m // tm, n_cols // tn),
        in_specs=[pl.BlockSpec((tm, kdim), lambda i, j: (i, 0)),
                  pl.BlockSpec((tn, kdim), lambda i, j: (j + joff, 0))],
        out_specs=pl.BlockSpec((tm, tn), lambda i, j: (i, j)),
        out_shape=jax.ShapeDtypeStruct((m, n_cols), out_dtype),
        compiler_params=_params("parallel", "parallel"),
        name=name,
    )(x, w)


def _cast_mm_kernel(x_ref, w_ref, z_ref, wb_ref, *, nk, w_is_nk):
    wb = w_ref[...].astype(BF16)
    wb_ref[...] = wb
    if w_is_nk:
        part = lax.dot_general(x_ref[...], wb, CONTRACT_LAST, preferred_element_type=F32)
    else:
        part = jnp.dot(x_ref[...], wb, preferred_element_type=F32)
    if nk == 1:
        z_ref[...] = part
        return
    k = pl.program_id(1)

    @pl.when(k == 0)
    def _():
        z_ref[...] = part

    @pl.when(k > 0)
    def _():
        z_ref[...] += part


def _cast_matmul(x, w_all, layer, *, n_cols=None, w_is_nk=False, name="cast_mm"):
    m, kdim = x.shape
    n_cols = w_all.shape[1 if w_is_nk else 2] if n_cols is None else n_cols
    tk = kdim if kdim <= 4096 else _pick(kdim, (5504, 4096, 2048, 1024, 512, 256, 128))
    tn = _pick(n_cols, (512, 256, 128))
    nk = kdim // tk
    if w_is_nk:
        w_spec = pl.BlockSpec((None, tn, tk), lambda j, k: (layer, j, k))
        wb_spec = pl.BlockSpec((tn, tk), lambda j, k: (j, k))
        wb_shape = (n_cols, kdim)
    else:
        w_spec = pl.BlockSpec((None, tk, tn), lambda j, k: (layer, k, j))
        wb_spec = pl.BlockSpec((None, tk, tn), lambda j, k: (j, k, 0))
        wb_shape = (n_cols // tn, kdim, tn)
    return pl.pallas_call(
        functools.partial(_cast_mm_kernel, nk=nk, w_is_nk=w_is_nk),
        grid=(n_cols // tn, nk),
        in_specs=[pl.BlockSpec((m, tk), lambda j, k: (0, k)), w_spec],
        out_specs=[pl.BlockSpec((m, tn), lambda j, k: (0, j)), wb_spec],
        out_shape=[jax.ShapeDtypeStruct((m, n_cols), F32), jax.ShapeDtypeStruct(wb_shape, BF16)],
        compiler_params=_params("parallel", "arbitrary"),
        name=name,
    )(x, w_all)


def _logf_kernel(x_ref, w_ref, b_ref, o_ref):
    z = jnp.dot(x_ref[...], w_ref[...], preferred_element_type=F32) + b_ref[...]
    o_ref[...] = jnp.minimum(z, 0.0) - jnp.log1p(jnp.exp(-jnp.abs(z)))


def _logf(x, w_f, b_f):
    m, kdim = x.shape
    tm = _pick(m, (512, 256, 128))
    return pl.pallas_call(
        _logf_kernel,
        grid=(m // tm,),
        in_specs=[pl.BlockSpec((tm, kdim), lambda i: (i, 0)),
                  pl.BlockSpec((kdim, HEAD_DIM), lambda i: (0, 0)),
                  pl.BlockSpec((1, HEAD_DIM), lambda i: (0, 0))],
        out_specs=pl.BlockSpec((tm, HEAD_DIM), lambda i: (i, 0)),
        out_shape=jax.ShapeDtypeStruct((m, HEAD_DIM), F32),
        compiler_params=_params("parallel"),
        name="logf",
    )(x, w_f, b_f)


def _head_rms(y, g):
    return y * lax.rsqrt(jnp.mean(y * y, axis=-1, keepdims=True) + EPS) * g


def _shift_rows(c, prev):
    row = lax.broadcasted_iota(jnp.int32, c.shape, 0)
    p1 = prev[SUBLANES - 1:SUBLANES, :]
    p2 = prev[SUBLANES - 2:SUBLANES - 1, :]
    c1 = jnp.where(row == 0, p1, pltpu.roll(c, 1, axis=0))
    c2 = jnp.where(row == 0, p2, jnp.where(row == 1, p1, pltpu.roll(c, 2, axis=0)))
    return c1, c2


def _sgu_kernel(u_ref, v_ref, lng_ref, lnb_ref, w_ref, bst_ref, g_ref, y_ref, vln_ref, *, n_chunks, n_heads):
    u = jax.nn.gelu(u_ref[...].astype(F32))
    v = jax.nn.gelu(v_ref[...].astype(F32))
    mu = jnp.mean(v, axis=-1, keepdims=True)
    d = v - mu
    var = jnp.mean(d * d, axis=-1, keepdims=True)
    vln = d * lax.rsqrt(var + EPS) * lng_ref[...] + lnb_ref[...]
    vln_ref[...] = vln
    vb = vln.astype(BF16)
    t_idx = lax.broadcasted_iota(jnp.int32, (CHUNK, CHUNK), 0)
    s_idx = lax.broadcasted_iota(jnp.int32, (CHUNK, CHUNK), 1)
    causal = s_idx <= t_idx
    for h in range(n_heads):
        cols = slice(h * HEAD_DIM, (h + 1) * HEAD_DIM)
        wm = jnp.where(causal, w_ref[h], 0.0).astype(BF16)
        bias = bst_ref[:, h:h + 1]
        g = g_ref[:, cols]
        for c in range(n_chunks):
            rows = slice(c * CHUNK, (c + 1) * CHUNK)
            mixed = jnp.dot(wm, vb[rows, cols], preferred_element_type=F32) + bias
            y_ref[rows, cols] = _head_rms(u[rows, cols] * mixed, g).astype(y_ref.dtype)


def _sgu(zb, ln_g, ln_b, w_s, b_s_t, norm_g, d_a):
    m = zb.shape[0]
    n_heads = d_a // HEAD_DIM
    tm = _pick(m, (256, 128))
    return pl.pallas_call(
        functools.partial(_sgu_kernel, n_chunks=tm // CHUNK, n_heads=n_heads),
        grid=(m // tm,),
        in_specs=[pl.BlockSpec((tm, d_a), lambda i: (i, 0)),
                  pl.BlockSpec((tm, d_a), lambda i: (i, 1)),
                  pl.BlockSpec((1, d_a), lambda i: (0, 0)),
                  pl.BlockSpec((1, d_a), lambda i: (0, 0)),
                  pl.BlockSpec((n_heads, CHUNK, CHUNK), lambda i: (0, 0, 0)),
                  pl.BlockSpec((CHUNK, n_heads), lambda i: (0, 0)),
                  pl.BlockSpec((1, d_a), lambda i: (0, 0))],
        out_specs=[pl.BlockSpec((tm, d_a), lambda i: (i, 0)),
                   pl.BlockSpec((tm, d_a), lambda i: (i, 0))],
        out_shape=[jax.ShapeDtypeStruct((m, d_a), BF16), jax.ShapeDtypeStruct((m, d_a), F32)],
        compiler_params=_params("parallel"),
        name="sgu",
    )(zb, zb, ln_g, ln_b, w_s, b_s_t, norm_g)


def _bconv_kernel(*refs, tiles_per_seq, n_heads, decode):
    if decode:
        x_ref, gb_ref, gc_ref, cw_ref, cb_ref, g_ref, p2_ref, p1_ref, y_ref, st_ref = refs
    else:
        x_ref, gb_ref, gc_ref, cw_ref, cb_ref, g_ref, y_ref, st_ref, carry_ref = refs
    c = gc_ref[...].astype(F32) * x_ref[...].astype(F32)
    tm = c.shape[0]
    if decode:
        c1, c2 = p1_ref[...], p2_ref[...]
    else:
        start = pl.program_id(0) % tiles_per_seq == 0
        prev = jnp.where(start, 0.0, carry_ref[...])
        c1, c2 = _shift_rows(c, prev)
        carry_ref[...] = c[tm - SUBLANES:, :]
    st_ref[...] = c[tm - SUBLANES:, :]
    conv = cb_ref[...] + cw_ref[0:1, :] * c2 + cw_ref[1:2, :] * c1 + cw_ref[2:3, :] * c
    y = gb_ref[...].astype(F32) * conv
    for h in range(n_heads):
        cols = slice(h * HEAD_DIM, (h + 1) * HEAD_DIM)
        y_ref[:, cols] = _head_rms(y[:, cols], g_ref[:, cols]).astype(y_ref.dtype)


def _bconv(zb, conv_w, conv_b, norm_g, d_b, col_blk, seq_len, state=None):
    m = zb.shape[0]
    decode = state is not None
    tm = m if decode else _pick(seq_len, (512, 256, 128))
    n_tiles = m // tm
    row = lambda i: (i, 0)
    fixed = lambda i: (0, 0)
    in_specs = [pl.BlockSpec((tm, d_b), lambda i: (i, col_blk)),
                pl.BlockSpec((tm, d_b), lambda i: (i, col_blk + 1)),
                pl.BlockSpec((tm, d_b), lambda i: (i, col_blk + 2)),
                pl.BlockSpec((CONV_W, d_b), fixed),
                pl.BlockSpec((1, d_b), fixed),
                pl.BlockSpec((1, d_b), fixed)]
    args = [zb, zb, zb, conv_w, conv_b, norm_g]
    scratch = []
    if decode:
        in_specs += [pl.BlockSpec((tm, d_b), row), pl.BlockSpec((tm, d_b), row)]
        args += [state[:, 0, :], state[:, 1, :]]
    else:
        scratch = [pltpu.VMEM((SUBLANES, d_b), F32)]
    return pl.pallas_call(
        functools.partial(_bconv_kernel, tiles_per_seq=max(seq_len // tm, 1), n_heads=d_b // HEAD_DIM,
                          decode=decode),
        grid=(n_tiles,),
        in_specs=in_specs,
        out_specs=[pl.BlockSpec((tm, d_b), row),
                   pl.BlockSpec((None, SUBLANES, d_b), lambda i: (i, 0, 0))],
        out_shape=[jax.ShapeDtypeStruct((m, d_b), BF16),
                   jax.ShapeDtypeStruct((n_tiles, SUBLANES, d_b), F32)],
        scratch_shapes=scratch,
        compiler_params=_params("arbitrary"),
        name="bconv_decode" if decode else "bconv",
    )(*args)


def _cumsum_kernel(x_ref, o_ref):
    x = x_ref[...]
    t = x.shape[-1]
    lane = lax.broadcasted_iota(jnp.int32, x.shape, 1)
    shift = 1
    while shift < t:
        x = x + jnp.where(lane >= shift, pltpu.roll(x, shift, axis=1), 0.0)
        shift *= 2
    o_ref[...] = x


def _cumsum_time(logf_t):
    b, h, t = logf_t.shape
    return pl.pallas_call(
        _cumsum_kernel,
        grid=(b,),
        in_specs=[pl.BlockSpec((None, h, t), lambda i: (i, 0, 0))],
        out_specs=pl.BlockSpec((None, h, t), lambda i: (i, 0, 0)),
        out_shape=jax.ShapeDtypeStruct((b, h, t), F32),
        compiler_params=_params("parallel"),
        name="cumsum_logf",
    )(logf_t)


def _fox_prompt_kernel(q_ref, k_ref, v_ref, fq_ref, fk_ref, g_ref, o_ref, *, tq, nq):
    scale = HEAD_DIM ** -0.5 * LOG2_E
    kb = k_ref[...].astype(BF16)
    vb = v_ref[...].astype(BF16)
    fk_all = fk_ref[...] * LOG2_E
    causal = (lax.broadcasted_iota(jnp.int32, (tq, tq), 1) <= lax.broadcasted_iota(jnp.int32, (tq, tq), 0))
    for qi in range(nq):
        rows = slice(qi * tq, (qi + 1) * tq)
        q = q_ref[rows, :]
        fq = fq_ref[rows, :] * LOG2_E
        sd = lax.dot_general(q, kb[rows, :], CONTRACT_LAST, preferred_element_type=F32)
        sd = jnp.where(causal, sd * scale + fq - fk_all[:, rows], MASK_VALUE)
        m = jnp.max(sd, axis=-1, keepdims=True)
        if qi > 0:
            past = slice(0, qi * tq)
            so = lax.dot_general(q, kb[past, :], CONTRACT_LAST, preferred_element_type=F32)
            so = so * scale + fq - fk_all[:, past]
            m = jnp.maximum(m, jnp.max(so, axis=-1, keepdims=True))
            po = jnp.exp2(so - m)
            l = jnp.sum(po, axis=-1, keepdims=True)
            acc = jnp.dot(po.astype(BF16), vb[past, :], preferred_element_type=F32)
        pd = jnp.exp2(sd - m)
        pv = jnp.dot(pd.astype(BF16), vb[rows, :], preferred_element_type=F32)
        if qi > 0:
            l = l + jnp.sum(pd, axis=-1, keepdims=True)
            acc = acc + pv
        else:
            l = jnp.sum(pd, axis=-1, keepdims=True)
            acc = pv
        o_ref[rows, :] = _head_rms(acc / l, g_ref[...]).astype(o_ref.dtype)


def _fox_prompt(zb, k, v, f_col, f_row, norm_g, *, batch, seq_len, n_heads, q_col_blk):
    m = zb.shape[0]
    tq = _pick(seq_len, (512, 256, 128))
    seq_blk = lambda b, h: (b, h)
    return pl.pallas_call(
        functools.partial(_fox_prompt_kernel, tq=tq, nq=seq_len // tq),
        grid=(batch, n_heads),
        in_specs=[pl.BlockSpec((seq_len, HEAD_DIM), lambda b, h: (b, q_col_blk + h)),
                  pl.BlockSpec((seq_len, HEAD_DIM), seq_blk),
                  pl.BlockSpec((seq_len, HEAD_DIM), seq_blk),
                  pl.BlockSpec((None, None, seq_len, 1), lambda b, h: (b, h, 0, 0)),
                  pl.BlockSpec((None, None, 1, seq_len), lambda b, h: (b, h, 0, 0)),
                  pl.BlockSpec((1, HEAD_DIM), lambda b, h: (0, h))],
        out_specs=pl.BlockSpec((seq_len, HEAD_DIM), seq_blk),
        out_shape=jax.ShapeDtypeStruct((m, n_heads * HEAD_DIM), BF16),
        compiler_params=_params("parallel", "parallel"),
        name="fox_prompt",
    )(zb, k, v, f_col, f_row, norm_g)


def _fox_decode_kernel(pt_ref, q_ref, kn_ref, vn_ref, lfn_ref, g_ref, *rest, n_heads, page):
    del pt_ref
    pps = PAGES_PER_STEP
    k_refs, v_refs, lf_refs = rest[:pps], rest[pps:2 * pps], rest[2 * pps:3 * pps]
    o_ref, m_ref, l_ref, acc_ref, carry_ref = rest[3 * pps:]
    step = pl.program_id(1)
    scale = HEAD_DIM ** -0.5
    n_rows = page * n_heads

    @pl.when(step == 0)
    def _():
        m_ref[...] = jnp.full(m_ref.shape, MASK_VALUE, F32)
        l_ref[...] = jnp.zeros(l_ref.shape, F32)
        acc_ref[...] = jnp.zeros(acc_ref.shape, F32)
        carry_ref[...] = jnp.zeros(carry_ref.shape, F32)

    q = q_ref[...]
    lf = jnp.concatenate([r[...] for r in lf_refs], axis=0)
    lane = lax.broadcasted_iota(jnp.int32, lf.shape, 1)
    suffix, total = lf, lf
    shift = n_heads
    while shift < n_rows:
        suffix = suffix + jnp.where(lane < n_rows - shift, pltpu.roll(suffix, n_rows - shift, axis=1), 0.0)
        total = total + pltpu.roll(total, shift, axis=1)
        shift *= 2
    carry = carry_ref[...]
    lf_new = lfn_ref[...]
    bias = [None] * pps
    for r in reversed(range(pps)):
        bias[r] = suffix[r:r + 1, :] - lf[r:r + 1, :] + carry + lf_new
        carry = carry + total[r:r + 1, :]
    carry_ref[...] = carry

    own_head = ((lax.broadcasted_iota(jnp.int32, (n_heads, n_rows), 1) & (n_heads - 1))
                == lax.broadcasted_iota(jnp.int32, (n_heads, n_rows), 0))
    contract_last = (((1,), (1,)), ((), ()))
    s = [jnp.where(own_head,
                   lax.dot_general(q, k_refs[r][...].astype(BF16), contract_last,
                                   preferred_element_type=F32) * scale + bias[r],
                   MASK_VALUE) for r in range(pps)]
    m_prev = m_ref[...]
    m_new = m_prev
    for r in range(pps):
        m_new = jnp.maximum(m_new, jnp.max(s[r], axis=-1, keepdims=True))
    alpha = jnp.exp(m_prev - m_new)
    l_new = alpha * l_ref[...]
    acc = alpha * acc_ref[...]
    for r in range(pps):
        p = jnp.exp(s[r] - m_new)
        l_new = l_new + jnp.sum(p, axis=-1, keepdims=True)
        acc = acc + jnp.dot(p.astype(BF16), v_refs[r][...].astype(BF16), preferred_element_type=F32)
    l_ref[...] = l_new
    acc_ref[...] = acc
    m_ref[...] = m_new

    @pl.when(step == pl.num_programs(1) - 1)
    def _():
        kn = kn_ref[...].astype(BF16).astype(F32)
        vn = vn_ref[...].astype(BF16).astype(F32)
        s_self = jnp.sum(q.astype(F32) * kn, axis=-1, keepdims=True) * scale
        m_prev = m_ref[...]
        m_fin = jnp.maximum(m_prev, s_self)
        a = jnp.exp(m_prev - m_fin)
        p_self = jnp.exp(s_self - m_fin)
        l_fin = a * l_ref[...] + p_self
        o = (a * acc_ref[...] + p_self.astype(BF16).astype(F32) * vn) / l_fin
        o_ref[...] = _head_rms(o, g_ref[...]).astype(o_ref.dtype)


def _fox_decode(q, k_new, v_new, lf_new, norm_g, cache_k, cache_v, cache_lf, page_table, layer):
    bsz, n_heads, _ = q.shape
    n_rows = cache_k.shape[2]
    page = n_rows // n_heads
    n_pages = page_table.shape[1]
    pps = PAGES_PER_STEP
    assert n_pages % pps == 0 and n_heads & (n_heads - 1) == 0
    n_steps = n_pages // pps

    def page_idx(b, s, pt, r):
        return pt[b, (n_steps - 1 - s) * pps + r]

    per_b = lambda b, s, pt: (b, 0, 0)
    kv_spec = lambda r: pl.BlockSpec((None, None, n_rows, HEAD_DIM),
                                     lambda b, s, pt: (layer, page_idx(b, s, pt, r), 0, 0))
    lf_spec = lambda r: pl.BlockSpec((None, None, 1, n_rows),
                                     lambda b, s, pt: (layer, page_idx(b, s, pt, r), 0, 0))
    in_specs = ([pl.BlockSpec((None, n_heads, HEAD_DIM), per_b)] * 3
                + [pl.BlockSpec((None, 1, n_rows), per_b),
                   pl.BlockSpec((n_heads, HEAD_DIM), lambda b, s, pt: (0, 0))]
                + [kv_spec(r) for r in range(pps)] * 2
                + [lf_spec(r) for r in range(pps)])
    grid_spec = pltpu.PrefetchScalarGridSpec(
        num_scalar_prefetch=1,
        grid=(bsz, n_steps),
        in_specs=in_specs,
        out_specs=pl.BlockSpec((None, n_heads, HEAD_DIM), per_b),
        scratch_shapes=[pltpu.VMEM((n_heads, 1), F32), pltpu.VMEM((n_heads, 1), F32),
                        pltpu.VMEM((n_heads, HEAD_DIM), F32), pltpu.VMEM((1, n_rows), F32)])
    return pl.pallas_call(
        functools.partial(_fox_decode_kernel, n_heads=n_heads, page=page),
        grid_spec=grid_spec,
        out_shape=jax.ShapeDtypeStruct((bsz, n_heads, HEAD_DIM), BF16),
        compiler_params=_params("parallel", "arbitrary"),
        name="fox_decode",
    )(page_table, q, k_new, v_new, lf_new, norm_g,
      *([cache_k] * pps), *([cache_v] * pps), *([cache_lf] * pps))


def _proj_ln_kernel(*refs, n_lhs, nk, nj, alpha, tn, tn_out):
    lhs_refs, w_refs = refs[:n_lhs], refs[n_lhs:2 * n_lhs]
    res_ref, g_ref, b_ref, o_ref, ob_ref, acc_ref, mu_ref, rs_ref = refs[2 * n_lhs:]
    t = pl.program_id(1)
    n_mm = nk * nj
    cols = pl.ds(pl.multiple_of((t % nj) * tn, tn), tn)

    def product():
        out = jnp.dot(lhs_refs[0][...], w_refs[0][...], preferred_element_type=F32)
        for x_ref, w_ref in zip(lhs_refs[1:], w_refs[1:]):
            out += jnp.dot(x_ref[...], w_ref[...], preferred_element_type=F32)
        return out

    @pl.when(t < nj)
    def _():
        acc_ref[:, cols] = product() + alpha * res_ref[...]

    if nk > 1:
        @pl.when((t >= nj) & (t < n_mm))
        def _():
            acc_ref[:, cols] += product()

    @pl.when(t == n_mm)
    def _():
        def row_stats(r, carry):
            rows = pl.ds(pl.multiple_of(r * STAT_ROWS, STAT_ROWS), STAT_ROWS)
            y = acc_ref[rows, :]
            mu = jnp.mean(y, axis=-1, keepdims=True)
            d = y - mu
            mu_ref[rows, :] = mu
            rs_ref[rows, :] = lax.rsqrt(jnp.mean(d * d, axis=-1, keepdims=True) + EPS)
            return carry
        lax.fori_loop(0, acc_ref.shape[0] // STAT_ROWS, row_stats, 0)

    @pl.when(t >= n_mm)
    def _():
        ocols = pl.ds(pl.multiple_of((t - n_mm) * tn_out, tn_out), tn_out)
        o = (acc_ref[:, ocols] - mu_ref[...]) * rs_ref[...] * g_ref[...] + b_ref[...]
        o_ref[...] = o
        ob_ref[...] = o.astype(ob_ref.dtype)


def _proj_ln(lhs_list, w, w_row_blocks, resid, g, b, *, alpha, nk, tm, tn_out, name):
    m, n = resid.shape
    tn = w.shape[2]
    tn_out = _pick(n, (tn_out, 512, 256, 128))
    nj, n_out = n // tn, n // tn_out
    n_mm = nk * nj
    n_lhs = len(lhs_list)
    widths = [x.shape[1] // nk for x in lhs_list]
    phase = lambda t: jnp.minimum(t, n_mm - 1) // nj
    col = lambda t: jnp.minimum(t, n_mm - 1) % nj
    out_col = lambda t: jnp.maximum(t - n_mm, 0)
    in_specs = [pl.BlockSpec((tm, kw), lambda i, t: (i, phase(t))) for kw in widths]
    in_specs += [pl.BlockSpec((None, kw, tn), lambda i, t, rb=rb: (col(t), rb + phase(t), 0))
                 for kw, rb in zip(widths, w_row_blocks)]
    in_specs += [pl.BlockSpec((tm, tn), lambda i, t: (i, jnp.minimum(t, nj - 1))),
                 pl.BlockSpec((1, tn_out), lambda i, t: (0, out_col(t))),
                 pl.BlockSpec((1, tn_out), lambda i, t: (0, out_col(t)))]
    out_spec = pl.BlockSpec((tm, tn_out), lambda i, t: (i, out_col(t)))
    return pl.pallas_call(
        functools.partial(_proj_ln_kernel, n_lhs=n_lhs, nk=nk, nj=nj, alpha=alpha, tn=tn, tn_out=tn_out),
        grid=(m // tm, n_mm + n_out),
        in_specs=in_specs,
        out_specs=[out_spec, out_spec],
        out_shape=[jax.ShapeDtypeStruct((m, n), F32), jax.ShapeDtypeStruct((m, n), BF16)],
        scratch_shapes=[pltpu.VMEM((tm, n), F32), pltpu.VMEM((tm, 1), F32), pltpu.VMEM((tm, 1), F32)],
        compiler_params=_params("parallel", "arbitrary"),
        name=name,
    )(*lhs_list, *([w] * n_lhs), resid, g, b)


def _ln_res_kernel(x_ref, h_ref, g_ref, b_ref, o_ref, ob_ref, *, alpha):
    y = alpha * x_ref[...] + h_ref[...]
    mu = jnp.mean(y, axis=-1, keepdims=True)
    d = y - mu
    var = jnp.mean(d * d, axis=-1, keepdims=True)
    o = d * lax.rsqrt(var + EPS) * g_ref[...] + b_ref[...]
    o_ref[...] = o
    ob_ref[...] = o.astype(ob_ref.dtype)


def _ln_res(x, h, g, b, alpha):
    m, d = x.shape
    tm = _pick(m, (256, 128))
    row = lambda i: (i, 0)
    fixed = lambda i: (0, 0)
    return pl.pallas_call(
        functools.partial(_ln_res_kernel, alpha=alpha),
        grid=(m // tm,),
        in_specs=[pl.BlockSpec((tm, d), row), pl.BlockSpec((tm, d), row),
                  pl.BlockSpec((1, d), fixed), pl.BlockSpec((1, d), fixed)],
        out_specs=[pl.BlockSpec((tm, d), row), pl.BlockSpec((tm, d), row)],
        out_shape=[jax.ShapeDtypeStruct((m, d), F32), jax.ShapeDtypeStruct((m, d), BF16)],
        compiler_params=_params("parallel"),
        name="ln_res",
    )(x, h, g, b)


def _conv_taps(c2, c1, c, cw_ref, cb_ref):
    return cb_ref[...] + cw_ref[0:1, :] * c2 + cw_ref[1:2, :] * c1 + cw_ref[2:3, :] * c


def _ffn_up_kernel(x_ref, w_ref, cwg_ref, cwv_ref, cbg_ref, cbv_ref, h_ref, sg_ref, sv_ref,
                   raw_ref, carry_ref, *, tiles_per_seq, nf, n_tiles):
    s = pl.program_id(0)
    tm = x_ref.shape[0]
    tf = h_ref.shape[1]

    @pl.when(s == 0)
    def _():
        raw_ref[...] = jnp.zeros(raw_ref.shape, F32)

    cw = jnp.concatenate([cwg_ref[...], cwv_ref[...]], axis=1)
    cb = jnp.concatenate([cbg_ref[...], cbv_ref[...]], axis=1)
    conv = (cb + cw[0:1, :] * raw_ref[pl.ds(SUBLANES - 2, tm), :] + cw[1:2, :] * raw_ref[pl.ds(SUBLANES - 1, tm), :]
            + cw[2:3, :] * raw_ref[pl.ds(SUBLANES, tm), :])
    gate, val = conv[:, :tf], conv[:, tf:]
    h_ref[...] = (gate * jax.nn.sigmoid(gate) * val).astype(h_ref.dtype)
    sg_ref[...] = raw_ref[pl.ds(tm, SUBLANES), pl.ds(0, tf)]
    sv_ref[...] = raw_ref[pl.ds(tm, SUBLANES), pl.ds(tf, tf)]

    sc = jnp.minimum(s, n_tiles * nf - 1)
    j = sc % nf
    start = (sc // nf) % tiles_per_seq == 0
    u = jnp.dot(x_ref[...], w_ref[...], preferred_element_type=F32)
    raw_ref[pl.ds(0, SUBLANES), :] = jnp.where(start, 0.0, carry_ref[j])
    raw_ref[pl.ds(SUBLANES, tm), :] = u
    carry_ref[j] = u[tm - SUBLANES:, :]


def _ffn_up(xb, w_cat, conv_w, conv_b, seq_len):
    m, d = xb.shape
    nf, _, tf2 = w_cat.shape
    tf = tf2 // 2
    d_ff = nf * tf
    tm = _pick(seq_len, (1024, 512, 256, 128))
    n_tiles = m // tm
    last = n_tiles * nf - 1
    cur = lambda s: jnp.minimum(s, last)
    prev = lambda s: jnp.maximum(s - 1, 0)
    return pl.pallas_call(
        functools.partial(_ffn_up_kernel, tiles_per_seq=seq_len // tm, nf=nf, n_tiles=n_tiles),
        grid=(n_tiles * nf + 1,),
        in_specs=[pl.BlockSpec((tm, d), lambda s: (cur(s) // nf, 0)),
                  pl.BlockSpec((None, d, 2 * tf), lambda s: (cur(s) % nf, 0, 0)),
                  pl.BlockSpec((CONV_W, tf), lambda s: (0, prev(s) % nf)),
                  pl.BlockSpec((CONV_W, tf), lambda s: (0, prev(s) % nf + nf)),
                  pl.BlockSpec((1, tf), lambda s: (0, prev(s) % nf)),
                  pl.BlockSpec((1, tf), lambda s: (0, prev(s) % nf + nf))],
        out_specs=[pl.BlockSpec((tm, tf), lambda s: (prev(s) // nf, prev(s) % nf)),
                   pl.BlockSpec((None, SUBLANES, tf), lambda s: (prev(s) // nf, 0, prev(s) % nf)),
                   pl.BlockSpec((None, SUBLANES, tf), lambda s: (prev(s) // nf, 0, prev(s) % nf))],
        out_shape=[jax.ShapeDtypeStruct((m, d_ff), BF16),
                   jax.ShapeDtypeStruct((n_tiles, SUBLANES, d_ff), F32),
                   jax.ShapeDtypeStruct((n_tiles, SUBLANES, d_ff), F32)],
        scratch_shapes=[pltpu.VMEM((tm + SUBLANES, 2 * tf), F32), pltpu.VMEM((nf, SUBLANES, 2 * tf), F32)],
        compiler_params=_params("arbitrary"),
        name="ffn_up",
    )(xb, w_cat, conv_w, conv_w, conv_b, conv_b)


def _ffn_up_decode_kernel(x_ref, wg_ref, wv_ref, cwg_ref, cwv_ref, cbg_ref, cbv_ref,
                          pg2_ref, pg1_ref, pv2_ref, pv1_ref, h_ref, ug_ref, uv_ref, wcat_ref):
    x = x_ref[...]
    wg = wg_ref[...].astype(BF16)
    wv = wv_ref[...].astype(BF16)
    tf = wg.shape[1]
    wcat_ref[:, pl.ds(0, tf)] = wg
    wcat_ref[:, pl.ds(tf, tf)] = wv
    ug = jnp.dot(x, wg, preferred_element_type=F32)
    uv = jnp.dot(x, wv, preferred_element_type=F32)
    ug_ref[...] = ug
    uv_ref[...] = uv
    gate = _conv_taps(pg2_ref[...], pg1_ref[...], ug, cwg_ref, cbg_ref)
    val = _conv_taps(pv2_ref[...], pv1_ref[...], uv, cwv_ref, cbv_ref)
    h_ref[...] = (gate * jax.nn.sigmoid(gate) * val).astype(h_ref.dtype)


def _ffn_up_decode(xb, w_up_all, layer, conv_w, conv_b, state):
    m, d = xb.shape
    d_ff = w_up_all.shape[2] // 2
    tf = _pick(d_ff, (256, 128))
    nf = d_ff // tf
    gcol = lambda j: (0, j)
    vcol = lambda j: (0, j + nf)
    s0, s1 = state[:, 0, :], state[:, 1, :]
    return pl.pallas_call(
        _ffn_up_decode_kernel,
        grid=(nf,),
        in_specs=[pl.BlockSpec((m, d), lambda j: (0, 0)),
                  pl.BlockSpec((None, d, tf), lambda j: (layer, 0, j)),
                  pl.BlockSpec((None, d, tf), lambda j: (layer, 0, j + nf)),
                  pl.BlockSpec((CONV_W, tf), gcol), pl.BlockSpec((CONV_W, tf), vcol),
                  pl.BlockSpec((1, tf), gcol), pl.BlockSpec((1, tf), vcol),
                  pl.BlockSpec((m, tf), gcol), pl.BlockSpec((m, tf), gcol),
                  pl.BlockSpec((m, tf), vcol), pl.BlockSpec((m, tf), vcol)],
        out_specs=[pl.BlockSpec((m, tf), gcol), pl.BlockSpec((m, tf), gcol), pl.BlockSpec((m, tf), gcol),
                   pl.BlockSpec((None, d, 2 * tf), lambda j: (j, 0, 0))],
        out_shape=[jax.ShapeDtypeStruct((m, d_ff), BF16),
                   jax.ShapeDtypeStruct((m, d_ff), F32), jax.ShapeDtypeStruct((m, d_ff), F32),
                   jax.ShapeDtypeStruct((nf, d, 2 * tf), BF16)],
        compiler_params=_params("parallel"),
        name="ffn_up_decode",
    )(xb, w_up_all, w_up_all, conv_w, conv_w, conv_b, conv_b, s0, s1, s0, s1)


def _layer(x, xb, lw, mats, *, alpha, batch, seq_len, decode_ctx=None):
    (w_f, b_f, sgu_ln_g, sgu_ln_b, sgu_w, sgu_b_t, conv_w, conv_b, mix_norm_g, ln1_g, ln1_b,
     ffn_conv_w, ffn_conv_b, ln2_g, ln2_b) = lw
    m, d = x.shape
    n_heads = d // HEAD_DIM
    d_a = (n_heads // 4) * HEAD_DIM
    d_c = d - 2 * d_a
    h_c = d_c // HEAD_DIM
    assert d_c == 2 * d_a
    decode = decode_ctx is not None

    n_zb = 2 * d_a + 3 * d_a + d_c
    if decode:
        layer = decode_ctx["layer"]
        w_in_all, w_o_all, w_up_all, w_down_all = mats
        z, w_in = _cast_matmul(xb, w_in_all, layer, n_cols=n_zb + 2 * d_c, w_is_nk=True, name="in_proj_decode")
        zb, k, v = z[:, :n_zb].astype(BF16), z[:, n_zb:n_zb + d_c], z[:, n_zb + d_c:]
    else:
        w_in, w_o, w_cat, w_down = mats
        zb = _matmul_nk(xb, w_in, col_off=0, n_cols=n_zb, out_dtype=BF16, name="in_proj_mix")
        k = _matmul_nk(xb, w_in, col_off=n_zb, n_cols=d_c, out_dtype=F32, name="in_proj_k")
        v = _matmul_nk(xb, w_in, col_off=n_zb + d_c, n_cols=d_c, out_dtype=F32, name="in_proj_v")
    logf = _logf(xb, w_f, b_f)[:, :h_c]

    g_a, g_b, g_c = mix_norm_g[:, :d_a], mix_norm_g[:, d_a:2 * d_a], mix_norm_g[:, 2 * d_a:]

    if decode:
        zuv = jnp.zeros((m, CHUNK, 2 * d_a), BF16).at[:, 0, :].set(zb[:, :2 * d_a]).reshape(m * CHUNK, 2 * d_a)
        ya_full, vln_full = _sgu(zuv, sgu_ln_g, sgu_ln_b, sgu_w, sgu_b_t, g_a, d_a)
        ya = ya_full.reshape(m, CHUNK, d_a)[:, 0, :]
        sgu_v = vln_full.reshape(m, CHUNK, d_a)[:, :1, :]
        yb, c_rows = _bconv(zb, conv_w, conv_b, g_b, d_a, 2, 1, state=decode_ctx["conv"])
        conv_state = jnp.stack([decode_ctx["conv"][:, 1, :], c_rows[0]], axis=1)
        q3 = zb[:, 5 * d_a:].reshape(m, h_c, HEAD_DIM)
        page = decode_ctx["cache_k"].shape[2] // h_c
        lf_rows = jnp.tile(logf, (1, page))[:, None, :]
        yc = _fox_decode(q3, k.reshape(m, h_c, HEAD_DIM), v.reshape(m, h_c, HEAD_DIM), lf_rows,
                         g_c.reshape(h_c, HEAD_DIM), decode_ctx["cache_k"], decode_ctx["cache_v"],
                         decode_ctx["cache_lf"], decode_ctx["page_table"], decode_ctx["layer"]).reshape(m, d_c)
        kh, vh = k.reshape(m, 1, h_c, HEAD_DIM), v.reshape(m, 1, h_c, HEAD_DIM)
        logf_out = logf.reshape(m, 1, h_c)
    else:
        ya, vln = _sgu(zb, sgu_ln_g, sgu_ln_b, sgu_w, sgu_b_t, g_a, d_a)
        last_chunk = ((seq_len - 1) // CHUNK) * CHUNK
        sgu_v = vln.reshape(batch, seq_len, d_a)[:, last_chunk:, :]
        yb, c_rows = _bconv(zb, conv_w, conv_b, g_b, d_a, 2, seq_len)
        tiles = c_rows.shape[0] // batch
        conv_state = c_rows.reshape(batch, tiles, SUBLANES, d_a)[:, -1, SUBLANES - (CONV_W - 1):, :]
        logf_b = logf.reshape(batch, seq_len, h_c)
        f_cum = _cumsum_time(jnp.transpose(logf_b, (0, 2, 1)))
        yc = _fox_prompt(zb, k, v, f_cum[:, :, :, None], f_cum[:, :, None, :], g_c,
                         batch=batch, seq_len=seq_len, n_heads=h_c, q_col_blk=5 * d_a // HEAD_DIM)
        kh, vh = k.reshape(batch, seq_len, h_c, HEAD_DIM), v.reshape(batch, seq_len, h_c, HEAD_DIM)
        logf_out = logf_b

    if decode:
        h, w_o = _cast_matmul(jnp.concatenate([ya, yb, yc], axis=-1), w_o_all, layer, name="oproj_decode")
        x1, x1b = _ln_res(x, h, ln1_g, ln1_b, alpha)
        hb, up_g, up_v, w_cat = _ffn_up_decode(x1b, w_up_all, layer, ffn_conv_w, ffn_conv_b,
                                                       decode_ctx["ffn"])
        up_rows = jnp.concatenate([up_g, up_v], axis=-1)
        ffn_state = jnp.stack([decode_ctx["ffn"][:, 1, :], up_rows], axis=1)
        ff, w_down = _cast_matmul(hb, w_down_all, layer, name="down_proj_decode")
        x2, x2b = _ln_res(x1, ff, ln2_g, ln2_b, alpha)
        made = (w_in, w_o, w_cat, w_down)
    else:
        x1, x1b = _proj_ln([ya, yb, yc], w_o, [0, 1, 1], x, ln1_g, ln1_b, alpha=alpha, nk=1,
                           tm=_pick(m, (1024, 512, 256, 128)), tn_out=512, name="oproj_ln")
        hb, up_g, up_v = _ffn_up(x1b, w_cat, ffn_conv_w, ffn_conv_b, seq_len)
        tiles = up_g.shape[0] // batch
        up_rows = jnp.concatenate([up_g, up_v], axis=-1)
        ffn_state = up_rows.reshape(batch, tiles, SUBLANES, -1)[:, -1, SUBLANES - (CONV_W - 1):, :]
        d_ff = hb.shape[1]
        nk = 2 if (d_ff // 2) % LANES == 0 else 1
        x2, x2b = _proj_ln([hb], w_down, [0], x1, ln2_g, ln2_b, alpha=alpha, nk=nk,
                           tm=_pick(m, (512, 256, 128)), tn_out=2048, name="down_ln")
        made = None
    return x2, x2b, (kh, vh, logf_out, conv_state, sgu_v, ffn_state), made


def kernel(x_prompt, x_sample, cache_k, cache_v, cache_logf, page_table, state_conv, state_ffn_conv,
           w_in, b_f, sgu_ln_g, sgu_ln_b, sgu_w, sgu_b, conv_w, conv_b, mix_norm_g, w_o,
           ln1_g, ln1_b, w_up, ffn_conv_w, ffn_conv_b, w_down, ln2_g, ln2_b):
    depth, d, d_in = w_in.shape
    batch, seq_len, _ = x_prompt.shape
    dec_batch, dec_seq, _ = x_sample.shape
    assert dec_seq == 1, "the sample group is one new token per sequence"
    alpha = (2 * depth) ** 0.25
    h_c = b_f.shape[1]
    n_main = d_in - h_c

    w_in_nk = jnp.swapaxes(w_in, 1, 2)
    w_f = jnp.pad(w_in[:, :, n_main:], ((0, 0), (0, 0), (0, HEAD_DIM - h_c))).astype(BF16)
    b_f_p = jnp.pad(b_f, ((0, 0), (0, HEAD_DIM - h_c)))[:, None, :]
    sgu_b_t = jnp.transpose(sgu_b, (0, 2, 1))
    n_pool, page = cache_k.shape[1], cache_k.shape[2]
    cache_k2 = cache_k.reshape(depth, n_pool, page * h_c, HEAD_DIM)
    cache_v2 = cache_v.reshape(depth, n_pool, page * h_c, HEAD_DIM)
    cache_lf = cache_logf.reshape(depth, n_pool, 1, page * h_c)
    row = lambda a: a[:, None, :]

    xp = x_prompt.reshape(batch * seq_len, d)
    xs = x_sample.reshape(dec_batch, d)
    xpb, xsb = xp.astype(BF16), xs.astype(BF16)
    sp, ss = [], []
    for l in range(depth):
        lw = (w_f[l], b_f_p[l], row(sgu_ln_g)[l], row(sgu_ln_b)[l], sgu_w[l], sgu_b_t[l],
              conv_w[l], row(conv_b)[l], row(mix_norm_g)[l], row(ln1_g)[l], row(ln1_b)[l],
              ffn_conv_w[l], row(ffn_conv_b)[l], row(ln2_g)[l], row(ln2_b)[l])
        ctx = dict(conv=state_conv[l], ffn=state_ffn_conv[l], cache_k=cache_k2, cache_v=cache_v2,
                   cache_lf=cache_lf, page_table=page_table, layer=l)
        xs, xsb, st, mats = _layer(xs, xsb, lw, (w_in_nk, w_o, w_up, w_down), alpha=alpha, batch=dec_batch,
                                   seq_len=1, decode_ctx=ctx)
        ss.append(st)
        xp, xpb, st, _ = _layer(xp, xpb, lw, mats, alpha=alpha, batch=batch, seq_len=seq_len)
        sp.append(st)

    def stack(sts, i):
        return jnp.stack([s[i] for s in sts])

    return (xp.reshape(batch, seq_len, d), xs.reshape(dec_batch, 1, d),
            stack(sp, 0), stack(sp, 1), stack(sp, 2), stack(sp, 3), stack(sp, 5), stack(sp, 4),
            stack(ss, 0), stack(ss, 1), stack(ss, 2), stack(ss, 3), stack(ss, 5), stack(ss, 4))
```

```python
import functools

import jax
import jax.numpy as jnp
from jax import lax
from jax.experimental import pallas as pl
from jax.experimental.pallas import tpu as pltpu

HEAD_DIM = 128
CHUNK = 128
CONV_W = 3
EPS = 1e-5
MASK_VALUE = -1e30
LOG2_E = 1.4426950408889634
SUBLANES = 8
LANES = 128
STAT_ROWS = 128
VMEM_LIMIT_BYTES = 56 * 1024 * 1024
PAGES_PER_STEP = 8

F32 = jnp.float32
BF16 = jnp.bfloat16


def _pick(dim, candidates):
    for c in candidates:
        if dim % c == 0:
            return c
    return dim


def _params(*semantics):
    return pltpu.CompilerParams(dimension_semantics=semantics, vmem_limit_bytes=VMEM_LIMIT_BYTES)


CONTRACT_LAST = (((1,), (1,)), ((), ()))


def _mm_nk_kernel(x_ref, w_ref, o_ref):
    o_ref[...] = lax.dot_general(x_ref[...], w_ref[...], CONTRACT_LAST,
                                 preferred_element_type=F32).astype(o_ref.dtype)


def _matmul_nk(x, w, *, col_off=0, n_cols=None, out_dtype=F32, name="mm"):
    m, kdim = x.shape
    n_cols = w.shape[0] if n_cols is None else n_cols
    tm = _pick(m, (1024, 512, 256, 128))
    tn = next(c for c in (1024, 512, 256, 128) if n_cols % c == 0 and col_off % c == 0)
    joff = col_off // tn
    return pl.pallas_call(
        _mm_nk_kernel,
        grid=(m // tm, n_cols // tn),
        in_specs=[pl.BlockSpec((tm, kdim), lambda i, j: (i, 0)),
                  pl.BlockSpec((tn, kdim), lambda i, j: (j + joff, 0))],
        out_specs=pl.BlockSpec((tm, tn), lambda i, j: (i, j)),
        out_shape=jax.ShapeDtypeStruct((m, n_cols), out_dtype),
        compiler_params=_params("parallel", "parallel"),
        name=name,
    )(x, w)


def _cast_mm_kernel(x_ref, w_ref, z_ref, wb_ref, *, nk, w_is_nk):
    wb = w_ref[...].astype(BF16)
    wb_ref[...] = wb
    if w_is_nk:
        part = lax.dot_general(x_ref[...], wb, CONTRACT_LAST, preferred_element_type=F32)
    else:
        part = jnp.dot(x_ref[...], wb, preferred_element_type=F32)
    if nk == 1:
        z_ref[...] = part
        return
    k = pl.program_id(1)

    @pl.when(k == 0)
    def _():
        z_ref[...] = part

    @pl.when(k > 0)
    def _():
        z_ref[...] += part


def _cast_matmul(x, w_all, layer, *, n_cols=None, w_is_nk=False, name="cast_mm"):
    m, kdim = x.shape
    n_cols = w_all.shape[1 if w_is_nk else 2] if n_cols is None else n_cols
    tk = kdim if kdim <= 4096 else _pick(kdim, (5504, 4096, 2048, 1024, 512, 256, 128))
    tn = _pick(n_cols, (512, 256, 128))
    nk = kdim // tk
    if w_is_nk:
        w_spec = pl.BlockSpec((None, tn, tk), lambda j, k: (layer, j, k))
        wb_spec = pl.BlockSpec((tn, tk), lambda j, k: (j, k))
        wb_shape = (n_cols, kdim)
    else:
        w_spec = pl.BlockSpec((None, tk, tn), lambda j, k: (layer, k, j))
        wb_spec = pl.BlockSpec((tk, tn), lambda j, k: (k, j))
        wb_shape = (kdim, n_cols)
    return pl.pallas_call(
        functools.partial(_cast_mm_kernel, nk=nk, w_is_nk=w_is_nk),
        grid=(n_cols // tn, nk),
        in_specs=[pl.BlockSpec((m, tk), lambda j, k: (0, k)), w_spec],
        out_specs=[pl.BlockSpec((m, tn), lambda j, k: (0, j)), wb_spec],
        out_shape=[jax.ShapeDtypeStruct((m, n_cols), F32), jax.ShapeDtypeStruct(wb_shape, BF16)],
        compiler_params=_params("parallel", "arbitrary"),
        name=name,
    )(x, w_all)


def _logf_kernel(x_ref, w_ref, b_ref, o_ref):
    z = jnp.dot(x_ref[...], w_ref[...], preferred_element_type=F32) + b_ref[...]
    o_ref[...] = jnp.minimum(z, 0.0) - jnp.log1p(jnp.exp(-jnp.abs(z)))


def _logf(x, w_f, b_f):
    m, kdim = x.shape
    tm = _pick(m, (512, 256, 128))
    return pl.pallas_call(
        _logf_kernel,
        grid=(m // tm,),
        in_specs=[pl.BlockSpec((tm, kdim), lambda i: (i, 0)),
                  pl.BlockSpec((kdim, HEAD_DIM), lambda i: (0, 0)),
                  pl.BlockSpec((1, HEAD_DIM), lambda i: (0, 0))],
        out_specs=pl.BlockSpec((tm, HEAD_DIM), lambda i: (i, 0)),
        out_shape=jax.ShapeDtypeStruct((m, HEAD_DIM), F32),
        compiler_params=_params("parallel"),
        name="logf",
    )(x, w_f, b_f)


def _head_rms(y, g):
    return y * lax.rsqrt(jnp.mean(y * y, axis=-1, keepdims=True) + EPS) * g


def _shift_rows(c, prev):
    row = lax.broadcasted_iota(jnp.int32, c.shape, 0)
    p1 = prev[SUBLANES - 1:SUBLANES, :]
    p2 = prev[SUBLANES - 2:SUBLANES - 1, :]
    c1 = jnp.where(row == 0, p1, pltpu.roll(c, 1, axis=0))
    c2 = jnp.where(row == 0, p2, jnp.where(row == 1, p1, pltpu.roll(c, 2, axis=0)))
    return c1, c2


def _sgu_kernel(u_ref, v_ref, lng_ref, lnb_ref, w_ref, bst_ref, g_ref, y_ref, vln_ref, *, n_chunks, n_heads):
    u = jax.nn.gelu(u_ref[...].astype(F32))
    v = jax.nn.gelu(v_ref[...].astype(F32))
    mu = jnp.mean(v, axis=-1, keepdims=True)
    d = v - mu
    var = jnp.mean(d * d, axis=-1, keepdims=True)
    vln = d * lax.rsqrt(var + EPS) * lng_ref[...] + lnb_ref[...]
    vln_ref[...] = vln
    vb = vln.astype(BF16)
    t_idx = lax.broadcasted_iota(jnp.int32, (CHUNK, CHUNK), 0)
    s_idx = lax.broadcasted_iota(jnp.int32, (CHUNK, CHUNK), 1)
    causal = s_idx <= t_idx
    for h in range(n_heads):
        cols = slice(h * HEAD_DIM, (h + 1) * HEAD_DIM)
        wm = jnp.where(causal, w_ref[h], 0.0).astype(BF16)
        bias = bst_ref[:, h:h + 1]
        g = g_ref[:, cols]
        for c in range(n_chunks):
            rows = slice(c * CHUNK, (c + 1) * CHUNK)
            mixed = jnp.dot(wm, vb[rows, cols], preferred_element_type=F32) + bias
            y_ref[rows, cols] = _head_rms(u[rows, cols] * mixed, g).astype(y_ref.dtype)


def _sgu(zb, ln_g, ln_b, w_s, b_s_t, norm_g, d_a):
    m = zb.shape[0]
    n_heads = d_a // HEAD_DIM
    tm = _pick(m, (256, 128))
    return pl.pallas_call(
        functools.partial(_sgu_kernel, n_chunks=tm // CHUNK, n_heads=n_heads),
        grid=(m // tm,),
        in_specs=[pl.BlockSpec((tm, d_a), lambda i: (i, 0)),
                  pl.BlockSpec((tm, d_a), lambda i: (i, 1)),
                  pl.BlockSpec((1, d_a), lambda i: (0, 0)),
                  pl.BlockSpec((1, d_a), lambda i: (0, 0)),
                  pl.BlockSpec((n_heads, CHUNK, CHUNK), lambda i: (0, 0, 0)),
                  pl.BlockSpec((CHUNK, n_heads), lambda i: (0, 0)),
                  pl.BlockSpec((1, d_a), lambda i: (0, 0))],
        out_specs=[pl.BlockSpec((tm, d_a), lambda i: (i, 0)),
                   pl.BlockSpec((tm, d_a), lambda i: (i, 0))],
        out_shape=[jax.ShapeDtypeStruct((m, d_a), BF16), jax.ShapeDtypeStruct((m, d_a), F32)],
        compiler_params=_params("parallel"),
        name="sgu",
    )(zb, zb, ln_g, ln_b, w_s, b_s_t, norm_g)


def _bconv_kernel(*refs, tiles_per_seq, n_heads, decode):
    if decode:
        x_ref, gb_ref, gc_ref, cw_ref, cb_ref, g_ref, p2_ref, p1_ref, y_ref, st_ref = refs
    else:
        x_ref, gb_ref, gc_ref, cw_ref, cb_ref, g_ref, y_ref, st_ref, carry_ref = refs
    c = gc_ref[...].astype(F32) * x_ref[...].astype(F32)
    tm = c.shape[0]
    if decode:
        c1, c2 = p1_ref[...], p2_ref[...]
    else:
        start = pl.program_id(0) % tiles_per_seq == 0
        prev = jnp.where(start, 0.0, carry_ref[...])
        c1, c2 = _shift_rows(c, prev)
        carry_ref[...] = c[tm - SUBLANES:, :]
    st_ref[...] = c[tm - SUBLANES:, :]
    conv = cb_ref[...] + cw_ref[0:1, :] * c2 + cw_ref[1:2, :] * c1 + cw_ref[2:3, :] * c
    y = gb_ref[...].astype(F32) * conv
    for h in range(n_heads):
        cols = slice(h * HEAD_DIM, (h + 1) * HEAD_DIM)
        y_ref[:, cols] = _head_rms(y[:, cols], g_ref[:, cols]).astype(y_ref.dtype)


def _bconv(zb, conv_w, conv_b, norm_g, d_b, col_blk, seq_len, state=None):
    m = zb.shape[0]
    decode = state is not None
    tm = m if decode else _pick(seq_len, (512, 256, 128))
    n_tiles = m // tm
    row = lambda i: (i, 0)
    fixed = lambda i: (0, 0)
    in_specs = [pl.BlockSpec((tm, d_b), lambda i: (i, col_blk)),
                pl.BlockSpec((tm, d_b), lambda i: (i, col_blk + 1)),
                pl.BlockSpec((tm, d_b), lambda i: (i, col_blk + 2)),
                pl.BlockSpec((CONV_W, d_b), fixed),
                pl.BlockSpec((1, d_b), fixed),
                pl.BlockSpec((1, d_b), fixed)]
    args = [zb, zb, zb, conv_w, conv_b, norm_g]
    scratch = []
    if decode:
        in_specs += [pl.BlockSpec((tm, d_b), row), pl.BlockSpec((tm, d_b), row)]
        args += [state[:, 0, :], state[:, 1, :]]
    else:
        scratch = [pltpu.VMEM((SUBLANES, d_b), F32)]
    return pl.pallas_call(
        functools.partial(_bconv_kernel, tiles_per_seq=max(seq_len // tm, 1), n_heads=d_b // HEAD_DIM,
                          decode=decode),
        grid=(n_tiles,),
        in_specs=in_specs,
        out_specs=[pl.BlockSpec((tm, d_b), row),
                   pl.BlockSpec((None, SUBLANES, d_b), lambda i: (i, 0, 0))],
        out_shape=[jax.ShapeDtypeStruct((m, d_b), BF16),
                   jax.ShapeDtypeStruct((n_tiles, SUBLANES, d_b), F32)],
        scratch_shapes=scratch,
        compiler_params=_params("arbitrary"),
        name="bconv_decode" if decode else "bconv",
    )(*args)


def _cumsum_kernel(x_ref, o_ref):
    x = x_ref[...]
    t = x.shape[-1]
    lane = lax.broadcasted_iota(jnp.int32, x.shape, 1)
    shift = 1
    while shift < t:
        x = x + jnp.where(lane >= shift, pltpu.roll(x, shift, axis=1), 0.0)
        shift *= 2
    o_ref[...] = x


def _cumsum_time(logf_t):
    b, h, t = logf_t.shape
    return pl.pallas_call(
        _cumsum_kernel,
        grid=(b,),
        in_specs=[pl.BlockSpec((None, h, t), lambda i: (i, 0, 0))],
        out_specs=pl.BlockSpec((None, h, t), lambda i: (i, 0, 0)),
        out_shape=jax.ShapeDtypeStruct((b, h, t), F32),
        compiler_params=_params("parallel"),
        name="cumsum_logf",
    )(logf_t)


def _fox_prompt_kernel(q_ref, k_ref, v_ref, fq_ref, fk_ref, g_ref, o_ref, *, tq, nq):
    scale = HEAD_DIM ** -0.5 * LOG2_E
    kb = k_ref[...].astype(BF16)
    vb = v_ref[...].astype(BF16)
    fk_all = fk_ref[...] * LOG2_E
    causal = (lax.broadcasted_iota(jnp.int32, (tq, tq), 1) <= lax.broadcasted_iota(jnp.int32, (tq, tq), 0))
    for qi in range(nq):
        rows = slice(qi * tq, (qi + 1) * tq)
        q = q_ref[rows, :]
        fq = fq_ref[rows, :] * LOG2_E
        sd = lax.dot_general(q, kb[rows, :], CONTRACT_LAST, preferred_element_type=F32)
        sd = jnp.where(causal, sd * scale + fq - fk_all[:, rows], MASK_VALUE)
        m = jnp.max(sd, axis=-1, keepdims=True)
        if qi > 0:
            past = slice(0, qi * tq)
            so = lax.dot_general(q, kb[past, :], CONTRACT_LAST, preferred_element_type=F32)
            so = so * scale + fq - fk_all[:, past]
            m = jnp.maximum(m, jnp.max(so, axis=-1, keepdims=True))
            po = jnp.exp2(so - m)
            l = jnp.sum(po, axis=-1, keepdims=True)
            acc = jnp.dot(po.astype(BF16), vb[past, :], preferred_element_type=F32)
        pd = jnp.exp2(sd - m)
        pv = jnp.dot(pd.astype(BF16), vb[rows, :], preferred_element_type=F32)
        if qi > 0:
            l = l + jnp.sum(pd, axis=-1, keepdims=True)
            acc = acc + pv
        else:
            l = jnp.sum(pd, axis=-1, keepdims=True)
            acc = pv
        o_ref[rows, :] = _head_rms(acc / l, g_ref[...]).astype(o_ref.dtype)


def _fox_prompt(zb, k, v, f_col, f_row, norm_g, *, batch, seq_len, n_heads, q_col_blk):
    m = zb.shape[0]
    tq = _pick(seq_len, (512, 256, 128))
    seq_blk = lambda b, h: (b, h)
    return pl.pallas_call(
        functools.partial(_fox_prompt_kernel, tq=tq, nq=seq_len // tq),
        grid=(batch, n_heads),
        in_specs=[pl.BlockSpec((seq_len, HEAD_DIM), lambda b, h: (b, q_col_blk + h)),
                  pl.BlockSpec((seq_len, HEAD_DIM), seq_blk),
                  pl.BlockSpec((seq_len, HEAD_DIM), seq_blk),
                  pl.BlockSpec((None, None, seq_len, 1), lambda b, h: (b, h, 0, 0)),
                  pl.BlockSpec((None, None, 1, seq_len), lambda b, h: (b, h, 0, 0)),
                  pl.BlockSpec((1, HEAD_DIM), lambda b, h: (0, h))],
        out_specs=pl.BlockSpec((seq_len, HEAD_DIM), seq_blk),
        out_shape=jax.ShapeDtypeStruct((m, n_heads * HEAD_DIM), BF16),
        compiler_params=_params("parallel", "parallel"),
        name="fox_prompt",
    )(zb, k, v, f_col, f_row, norm_g)


def _fox_decode_kernel(pt_ref, q_ref, kn_ref, vn_ref, lfn_ref, g_ref, *rest, n_heads, page):
    del pt_ref
    pps = PAGES_PER_STEP
    k_refs, v_refs, lf_refs = rest[:pps], rest[pps:2 * pps], rest[2 * pps:3 * pps]
    o_ref, m_ref, l_ref, acc_ref, carry_ref = rest[3 * pps:]
    step = pl.program_id(1)
    scale = HEAD_DIM ** -0.5
    n_rows = page * n_heads

    @pl.when(step == 0)
    def _():
        m_ref[...] = jnp.full(m_ref.shape, MASK_VALUE, F32)
        l_ref[...] = jnp.zeros(l_ref.shape, F32)
        acc_ref[...] = jnp.zeros(acc_ref.shape, F32)
        carry_ref[...] = jnp.zeros(carry_ref.shape, F32)

    q = q_ref[...]
    lf = jnp.concatenate([r[...] for r in lf_refs], axis=0)
    lane = lax.broadcasted_iota(jnp.int32, lf.shape, 1)
    suffix, total = lf, lf
    shift = n_heads
    while shift < n_rows:
        suffix = suffix + jnp.where(lane < n_rows - shift, pltpu.roll(suffix, n_rows - shift, axis=1), 0.0)
        total = total + pltpu.roll(total, shift, axis=1)
        shift *= 2
    carry = carry_ref[...]
    lf_new = lfn_ref[...]
    bias = [None] * pps
    for r in reversed(range(pps)):
        bias[r] = suffix[r:r + 1, :] - lf[r:r + 1, :] + carry + lf_new
        carry = carry + total[r:r + 1, :]
    carry_ref[...] = carry

    own_head = ((lax.broadcasted_iota(jnp.int32, (n_heads, n_rows), 1) & (n_heads - 1))
                == lax.broadcasted_iota(jnp.int32, (n_heads, n_rows), 0))
    contract_last = (((1,), (1,)), ((), ()))
    s = [jnp.where(own_head,
                   lax.dot_general(q, k_refs[r][...].astype(BF16), contract_last,
                                   preferred_element_type=F32) * scale + bias[r],
                   MASK_VALUE) for r in range(pps)]
    m_prev = m_ref[...]
    m_new = m_prev
    for r in range(pps):
        m_new = jnp.maximum(m_new, jnp.max(s[r], axis=-1, keepdims=True))
    alpha = jnp.exp(m_prev - m_new)
    l_new = alpha * l_ref[...]
    acc = alpha * acc_ref[...]
    for r in range(pps):
        p = jnp.exp(s[r] - m_new)
        l_new = l_new + jnp.sum(p, axis=-1, keepdims=True)
        acc = acc + jnp.dot(p.astype(BF16), v_refs[r][...].astype(BF16), preferred_element_type=F32)
    l_ref[...] = l_new
    acc_ref[...] = acc
    m_ref[...] = m_new

    @pl.when(step == pl.num_programs(1) - 1)
    def _():
        kn = kn_ref[...].astype(BF16).astype(F32)
        vn = vn_ref[...].astype(BF16).astype(F32)
        s_self = jnp.sum(q.astype(F32) * kn, axis=-1, keepdims=True) * scale
        m_prev = m_ref[...]
        m_fin = jnp.maximum(m_prev, s_self)
        a = jnp.exp(m_prev - m_fin)
        p_self = jnp.exp(s_self - m_fin)
        l_fin = a * l_ref[...] + p_self
        o = (a * acc_ref[...] + p_self.astype(BF16).astype(F32) * vn) / l_fin
        o_ref[...] = _head_rms(o, g_ref[...]).astype(o_ref.dtype)


def _fox_decode(q, k_new, v_new, lf_new, norm_g, cache_k, cache_v, cache_lf, page_table, layer):
    bsz, n_heads, _ = q.shape
    n_rows = cache_k.shape[2]
    page = n_rows // n_heads
    n_pages = page_table.shape[1]
    pps = PAGES_PER_STEP
    assert n_pages % pps == 0 and n_heads & (n_heads - 1) == 0
    n_steps = n_pages // pps

    def page_idx(b, s, pt, r):
        return pt[b, (n_steps - 1 - s) * pps + r]

    per_b = lambda b, s, pt: (b, 0, 0)
    kv_spec = lambda r: pl.BlockSpec((None, None, n_rows, HEAD_DIM),
                                     lambda b, s, pt: (layer, page_idx(b, s, pt, r), 0, 0))
    lf_spec = lambda r: pl.BlockSpec((None, None, 1, n_rows),
                                     lambda b, s, pt: (layer, page_idx(b, s, pt, r), 0, 0))
    in_specs = ([pl.BlockSpec((None, n_heads, HEAD_DIM), per_b)] * 3
                + [pl.BlockSpec((None, 1, n_rows), per_b),
                   pl.BlockSpec((n_heads, HEAD_DIM), lambda b, s, pt: (0, 0))]
                + [kv_spec(r) for r in range(pps)] * 2
                + [lf_spec(r) for r in range(pps)])
    grid_spec = pltpu.PrefetchScalarGridSpec(
        num_scalar_prefetch=1,
        grid=(bsz, n_steps),
        in_specs=in_specs,
        out_specs=pl.BlockSpec((None, n_heads, HEAD_DIM), per_b),
        scratch_shapes=[pltpu.VMEM((n_heads, 1), F32), pltpu.VMEM((n_heads, 1), F32),
                        pltpu.VMEM((n_heads, HEAD_DIM), F32), pltpu.VMEM((1, n_rows), F32)])
    return pl.pallas_call(
        functools.partial(_fox_decode_kernel, n_heads=n_heads, page=page),
        grid_spec=grid_spec,
        out_shape=jax.ShapeDtypeStruct((bsz, n_heads, HEAD_DIM), BF16),
        compiler_params=_params("parallel", "arbitrary"),
        name="fox_decode",
    )(page_table, q, k_new, v_new, lf_new, norm_g,
      *([cache_k] * pps), *([cache_v] * pps), *([cache_lf] * pps))


def _proj_ln_kernel(*refs, n_lhs, nk, nj, alpha, tn, tn_out):
    lhs_refs, w_refs = refs[:n_lhs], refs[n_lhs:2 * n_lhs]
    res_ref, g_ref, b_ref, o_ref, ob_ref, acc_ref, mu_ref, rs_ref = refs[2 * n_lhs:]
    t = pl.program_id(1)
    n_mm = nk * nj
    cols = pl.ds(pl.multiple_of((t % nj) * tn, tn), tn)

    def product():
        out = jnp.dot(lhs_refs[0][...], w_refs[0][...], preferred_element_type=F32)
        for x_ref, w_ref in zip(lhs_refs[1:], w_refs[1:]):
            out += jnp.dot(x_ref[...], w_ref[...], preferred_element_type=F32)
        return out

    @pl.when(t < nj)
    def _():
        acc_ref[:, cols] = product() + alpha * res_ref[...]

    if nk > 1:
        @pl.when((t >= nj) & (t < n_mm))
        def _():
            acc_ref[:, cols] += product()

    @pl.when(t == n_mm)
    def _():
        def row_stats(r, carry):
            rows = pl.ds(pl.multiple_of(r * STAT_ROWS, STAT_ROWS), STAT_ROWS)
            y = acc_ref[rows, :]
            mu = jnp.mean(y, axis=-1, keepdims=True)
            d = y - mu
            mu_ref[rows, :] = mu
            rs_ref[rows, :] = lax.rsqrt(jnp.mean(d * d, axis=-1, keepdims=True) + EPS)
            return carry
        lax.fori_loop(0, acc_ref.shape[0] // STAT_ROWS, row_stats, 0)

    @pl.when(t >= n_mm)
    def _():
        ocols = pl.ds(pl.multiple_of((t - n_mm) * tn_out, tn_out), tn_out)
        o = (acc_ref[:, ocols] - mu_ref[...]) * rs_ref[...] * g_ref[...] + b_ref[...]
        o_ref[...] = o
        ob_ref[...] = o.astype(ob_ref.dtype)


def _proj_ln(lhs_list, w, w_row_blocks, resid, g, b, *, alpha, nk, tm, tn, tn_out, name):
    m, n = resid.shape
    tn = _pick(n, (tn, 512, 256, 128))
    tn_out = _pick(n, (tn_out, 512, 256, 128))
    nj, n_out = n // tn, n // tn_out
    n_mm = nk * nj
    n_lhs = len(lhs_list)
    widths = [x.shape[1] // nk for x in lhs_list]
    phase = lambda t: jnp.minimum(t, n_mm - 1) // nj
    col = lambda t: jnp.minimum(t, n_mm - 1) % nj
    out_col = lambda t: jnp.maximum(t - n_mm, 0)
    in_specs = [pl.BlockSpec((tm, kw), lambda i, t: (i, phase(t))) for kw in widths]
    in_specs += [pl.BlockSpec((kw, tn), lambda i, t, rb=rb: (rb + phase(t), col(t)))
                 for kw, rb in zip(widths, w_row_blocks)]
    in_specs += [pl.BlockSpec((tm, tn), lambda i, t: (i, jnp.minimum(t, nj - 1))),
                 pl.BlockSpec((1, tn_out), lambda i, t: (0, out_col(t))),
                 pl.BlockSpec((1, tn_out), lambda i, t: (0, out_col(t)))]
    out_spec = pl.BlockSpec((tm, tn_out), lambda i, t: (i, out_col(t)))
    return pl.pallas_call(
        functools.partial(_proj_ln_kernel, n_lhs=n_lhs, nk=nk, nj=nj, alpha=alpha, tn=tn, tn_out=tn_out),
        grid=(m // tm, n_mm + n_out),
        in_specs=in_specs,
        out_specs=[out_spec, out_spec],
        out_shape=[jax.ShapeDtypeStruct((m, n), F32), jax.ShapeDtypeStruct((m, n), BF16)],
        scratch_shapes=[pltpu.VMEM((tm, n), F32), pltpu.VMEM((tm, 1), F32), pltpu.VMEM((tm, 1), F32)],
        compiler_params=_params("parallel", "arbitrary"),
        name=name,
    )(*lhs_list, *([w] * n_lhs), resid, g, b)


def _ln_res_kernel(x_ref, h_ref, g_ref, b_ref, o_ref, ob_ref, *, alpha):
    y = alpha * x_ref[...] + h_ref[...]
    mu = jnp.mean(y, axis=-1, keepdims=True)
    d = y - mu
    var = jnp.mean(d * d, axis=-1, keepdims=True)
    o = d * lax.rsqrt(var + EPS) * g_ref[...] + b_ref[...]
    o_ref[...] = o
    ob_ref[...] = o.astype(ob_ref.dtype)


def _ln_res(x, h, g, b, alpha):
    m, d = x.shape
    tm = _pick(m, (256, 128))
    row = lambda i: (i, 0)
    fixed = lambda i: (0, 0)
    return pl.pallas_call(
        functools.partial(_ln_res_kernel, alpha=alpha),
        grid=(m // tm,),
        in_specs=[pl.BlockSpec((tm, d), row), pl.BlockSpec((tm, d), row),
                  pl.BlockSpec((1, d), fixed), pl.BlockSpec((1, d), fixed)],
        out_specs=[pl.BlockSpec((tm, d), row), pl.BlockSpec((tm, d), row)],
        out_shape=[jax.ShapeDtypeStruct((m, d), F32), jax.ShapeDtypeStruct((m, d), BF16)],
        compiler_params=_params("parallel"),
        name="ln_res",
    )(x, h, g, b)


def _conv_taps(c2, c1, c, cw_ref, cb_ref):
    return cb_ref[...] + cw_ref[0:1, :] * c2 + cw_ref[1:2, :] * c1 + cw_ref[2:3, :] * c


def _ffn_up_kernel(x_ref, wg_ref, wv_ref, cwg_ref, cwv_ref, cbg_ref, cbv_ref, h_ref, sg_ref, sv_ref,
                   raw_g_ref, raw_v_ref, carry_g_ref, carry_v_ref, *, tiles_per_seq, nf, n_tiles):
    s = pl.program_id(0)
    tm = x_ref.shape[0]

    @pl.when(s == 0)
    def _():
        raw_g_ref[...] = jnp.zeros(raw_g_ref.shape, F32)
        raw_v_ref[...] = jnp.zeros(raw_v_ref.shape, F32)

    def taps(raw_ref, cw_ref, cb_ref):
        return _conv_taps(raw_ref[pl.ds(SUBLANES - 2, tm), :], raw_ref[pl.ds(SUBLANES - 1, tm), :],
                          raw_ref[pl.ds(SUBLANES, tm), :], cw_ref, cb_ref)

    gate = taps(raw_g_ref, cwg_ref, cbg_ref)
    val = taps(raw_v_ref, cwv_ref, cbv_ref)
    h_ref[...] = (gate * jax.nn.sigmoid(gate) * val).astype(h_ref.dtype)
    sg_ref[...] = raw_g_ref[pl.ds(tm, SUBLANES), :]
    sv_ref[...] = raw_v_ref[pl.ds(tm, SUBLANES), :]

    sc = jnp.minimum(s, n_tiles * nf - 1)
    j = sc % nf
    start = (sc // nf) % tiles_per_seq == 0
    x = x_ref[...]
    for w_ref, raw_ref, carry_ref in ((wg_ref, raw_g_ref, carry_g_ref), (wv_ref, raw_v_ref, carry_v_ref)):
        u = jnp.dot(x, w_ref[...], preferred_element_type=F32)
        raw_ref[pl.ds(0, SUBLANES), :] = jnp.where(start, 0.0, carry_ref[j])
        raw_ref[pl.ds(SUBLANES, tm), :] = u
        carry_ref[j] = u[tm - SUBLANES:, :]


def _ffn_up(xb, w_gate, w_val, conv_w, conv_b, seq_len):
    m, d = xb.shape
    d_ff = w_gate.shape[1]
    tf = _pick(d_ff, (256, 128))
    nf = d_ff // tf
    tm = _pick(seq_len, (1024, 512, 256, 128))
    n_tiles = m // tm
    last = n_tiles * nf - 1
    cur = lambda s: jnp.minimum(s, last)
    prev = lambda s: jnp.maximum(s - 1, 0)
    return pl.pallas_call(
        functools.partial(_ffn_up_kernel, tiles_per_seq=seq_len // tm, nf=nf, n_tiles=n_tiles),
        grid=(n_tiles * nf + 1,),
        in_specs=[pl.BlockSpec((tm, d), lambda s: (cur(s) // nf, 0)),
                  pl.BlockSpec((d, tf), lambda s: (0, cur(s) % nf)),
                  pl.BlockSpec((d, tf), lambda s: (0, cur(s) % nf)),
                  pl.BlockSpec((CONV_W, tf), lambda s: (0, prev(s) % nf)),
                  pl.BlockSpec((CONV_W, tf), lambda s: (0, prev(s) % nf + nf)),
                  pl.BlockSpec((1, tf), lambda s: (0, prev(s) % nf)),
                  pl.BlockSpec((1, tf), lambda s: (0, prev(s) % nf + nf))],
        out_specs=[pl.BlockSpec((tm, tf), lambda s: (prev(s) // nf, prev(s) % nf)),
                   pl.BlockSpec((None, SUBLANES, tf), lambda s: (prev(s) // nf, 0, prev(s) % nf)),
                   pl.BlockSpec((None, SUBLANES, tf), lambda s: (prev(s) // nf, 0, prev(s) % nf))],
        out_shape=[jax.ShapeDtypeStruct((m, d_ff), BF16),
                   jax.ShapeDtypeStruct((n_tiles, SUBLANES, d_ff), F32),
                   jax.ShapeDtypeStruct((n_tiles, SUBLANES, d_ff), F32)],
        scratch_shapes=[pltpu.VMEM((tm + SUBLANES, tf), F32), pltpu.VMEM((tm + SUBLANES, tf), F32),
                        pltpu.VMEM((nf, SUBLANES, tf), F32), pltpu.VMEM((nf, SUBLANES, tf), F32)],
        compiler_params=_params("arbitrary"),
        name="ffn_up",
    )(xb, w_gate, w_val, conv_w, conv_w, conv_b, conv_b)


def _ffn_up_decode_kernel(x_ref, wg_ref, wv_ref, cwg_ref, cwv_ref, cbg_ref, cbv_ref,
                          pg2_ref, pg1_ref, pv2_ref, pv1_ref, h_ref, ug_ref, uv_ref, wgb_ref, wvb_ref):
    x = x_ref[...]
    wg = wg_ref[...].astype(BF16)
    wv = wv_ref[...].astype(BF16)
    wgb_ref[...] = wg
    wvb_ref[...] = wv
    ug = jnp.dot(x, wg, preferred_element_type=F32)
    uv = jnp.dot(x, wv, preferred_element_type=F32)
    ug_ref[...] = ug
    uv_ref[...] = uv
    gate = _conv_taps(pg2_ref[...], pg1_ref[...], ug, cwg_ref, cbg_ref)
    val = _conv_taps(pv2_ref[...], pv1_ref[...], uv, cwv_ref, cbv_ref)
    h_ref[...] = (gate * jax.nn.sigmoid(gate) * val).astype(h_ref.dtype)


def _ffn_up_decode(xb, w_up_all, layer, conv_w, conv_b, state):
    m, d = xb.shape
    d_ff = w_up_all.shape[2] // 2
    tf = _pick(d_ff, (256, 128))
    nf = d_ff // tf
    gcol = lambda j: (0, j)
    vcol = lambda j: (0, j + nf)
    s0, s1 = state[:, 0, :], state[:, 1, :]
    return pl.pallas_call(
        _ffn_up_decode_kernel,
        grid=(nf,),
        in_specs=[pl.BlockSpec((m, d), lambda j: (0, 0)),
                  pl.BlockSpec((None, d, tf), lambda j: (layer, 0, j)),
                  pl.BlockSpec((None, d, tf), lambda j: (layer, 0, j + nf)),
                  pl.BlockSpec((CONV_W, tf), gcol), pl.BlockSpec((CONV_W, tf), vcol),
                  pl.BlockSpec((1, tf), gcol), pl.BlockSpec((1, tf), vcol),
                  pl.BlockSpec((m, tf), gcol), pl.BlockSpec((m, tf), gcol),
                  pl.BlockSpec((m, tf), vcol), pl.BlockSpec((m, tf), vcol)],
        out_specs=[pl.BlockSpec((m, tf), gcol), pl.BlockSpec((m, tf), gcol), pl.BlockSpec((m, tf), gcol),
                   pl.BlockSpec((d, tf), gcol), pl.BlockSpec((d, tf), gcol)],
        out_shape=[jax.ShapeDtypeStruct((m, d_ff), BF16),
                   jax.ShapeDtypeStruct((m, d_ff), F32), jax.ShapeDtypeStruct((m, d_ff), F32),
                   jax.ShapeDtypeStruct((d, d_ff), BF16), jax.ShapeDtypeStruct((d, d_ff), BF16)],
        compiler_params=_params("parallel"),
        name="ffn_up_decode",
    )(xb, w_up_all, w_up_all, conv_w, conv_w, conv_b, conv_b, s0, s1, s0, s1)


def _layer(x, xb, lw, mats, *, alpha, batch, seq_len, decode_ctx=None):
    (w_f, b_f, sgu_ln_g, sgu_ln_b, sgu_w, sgu_b_t, conv_w, conv_b, mix_norm_g, ln1_g, ln1_b,
     ffn_conv_w, ffn_conv_b, ln2_g, ln2_b) = lw
    m, d = x.shape
    n_heads = d // HEAD_DIM
    d_a = (n_heads // 4) * HEAD_DIM
    d_c = d - 2 * d_a
    h_c = d_c // HEAD_DIM
    assert d_c == 2 * d_a
    decode = decode_ctx is not None

    n_zb = 2 * d_a + 3 * d_a + d_c
    if decode:
        layer = decode_ctx["layer"]
        w_in_all, w_o_all, w_up_all, w_down_all = mats
        z, w_in = _cast_matmul(xb, w_in_all, layer, n_cols=n_zb + 2 * d_c, w_is_nk=True, name="in_proj_decode")
        zb, k, v = z[:, :n_zb].astype(BF16), z[:, n_zb:n_zb + d_c], z[:, n_zb + d_c:]
    else:
        w_in, w_o, w_gate, w_val, w_down = mats
        zb = _matmul_nk(xb, w_in, col_off=0, n_cols=n_zb, out_dtype=BF16, name="in_proj_mix")
        k = _matmul_nk(xb, w_in, col_off=n_zb, n_cols=d_c, out_dtype=F32, name="in_proj_k")
        v = _matmul_nk(xb, w_in, col_off=n_zb + d_c, n_cols=d_c, out_dtype=F32, name="in_proj_v")
    logf = _logf(xb, w_f, b_f)[:, :h_c]

    g_a, g_b, g_c = mix_norm_g[:, :d_a], mix_norm_g[:, d_a:2 * d_a], mix_norm_g[:, 2 * d_a:]

    if decode:
        zuv = jnp.zeros((m, CHUNK, 2 * d_a), BF16).at[:, 0, :].set(zb[:, :2 * d_a]).reshape(m * CHUNK, 2 * d_a)
        ya_full, vln_full = _sgu(zuv, sgu_ln_g, sgu_ln_b, sgu_w, sgu_b_t, g_a, d_a)
        ya = ya_full.reshape(m, CHUNK, d_a)[:, 0, :]
        sgu_v = vln_full.reshape(m, CHUNK, d_a)[:, :1, :]
        yb, c_rows = _bconv(zb, conv_w, conv_b, g_b, d_a, 2, 1, state=decode_ctx["conv"])
        conv_state = jnp.stack([decode_ctx["conv"][:, 1, :], c_rows[0]], axis=1)
        q3 = zb[:, 5 * d_a:].reshape(m, h_c, HEAD_DIM)
        page = decode_ctx["cache_k"].shape[2] // h_c
        lf_rows = jnp.tile(logf, (1, page))[:, None, :]
        yc = _fox_decode(q3, k.reshape(m, h_c, HEAD_DIM), v.reshape(m, h_c, HEAD_DIM), lf_rows,
                         g_c.reshape(h_c, HEAD_DIM), decode_ctx["cache_k"], decode_ctx["cache_v"],
                         decode_ctx["cache_lf"], decode_ctx["page_table"], decode_ctx["layer"]).reshape(m, d_c)
        kh, vh = k.reshape(m, 1, h_c, HEAD_DIM), v.reshape(m, 1, h_c, HEAD_DIM)
        logf_out = logf.reshape(m, 1, h_c)
    else:
        ya, vln = _sgu(zb, sgu_ln_g, sgu_ln_b, sgu_w, sgu_b_t, g_a, d_a)
        last_chunk = ((seq_len - 1) // CHUNK) * CHUNK
        sgu_v = vln.reshape(batch, seq_len, d_a)[:, last_chunk:, :]
        yb, c_rows = _bconv(zb, conv_w, conv_b, g_b, d_a, 2, seq_len)
        tiles = c_rows.shape[0] // batch
        conv_state = c_rows.reshape(batch, tiles, SUBLANES, d_a)[:, -1, SUBLANES - (CONV_W - 1):, :]
        logf_b = logf.reshape(batch, seq_len, h_c)
        f_cum = _cumsum_time(jnp.transpose(logf_b, (0, 2, 1)))
        yc = _fox_prompt(zb, k, v, f_cum[:, :, :, None], f_cum[:, :, None, :], g_c,
                         batch=batch, seq_len=seq_len, n_heads=h_c, q_col_blk=5 * d_a // HEAD_DIM)
        kh, vh = k.reshape(batch, seq_len, h_c, HEAD_DIM), v.reshape(batch, seq_len, h_c, HEAD_DIM)
        logf_out = logf_b

    if decode:
        h, w_o = _cast_matmul(jnp.concatenate([ya, yb, yc], axis=-1), w_o_all, layer, name="oproj_decode")
        x1, x1b = _ln_res(x, h, ln1_g, ln1_b, alpha)
        hb, up_g, up_v, w_gate, w_val = _ffn_up_decode(x1b, w_up_all, layer, ffn_conv_w, ffn_conv_b,
                                                       decode_ctx["ffn"])
        up_rows = jnp.concatenate([up_g, up_v], axis=-1)
        ffn_state = jnp.stack([decode_ctx["ffn"][:, 1, :], up_rows], axis=1)
        ff, w_down = _cast_matmul(hb, w_down_all, layer, name="down_proj_decode")
        x2, x2b = _ln_res(x1, ff, ln2_g, ln2_b, alpha)
        made = (w_in, w_o, w_gate, w_val, w_down)
    else:
        x1, x1b = _proj_ln([ya, yb, yc], w_o, [0, 1, 1], x, ln1_g, ln1_b, alpha=alpha, nk=1,
                           tm=_pick(m, (1024, 512, 256, 128)), tn=512, tn_out=512, name="oproj_ln")
        hb, up_g, up_v = _ffn_up(x1b, w_gate, w_val, ffn_conv_w, ffn_conv_b, seq_len)
        tiles = up_g.shape[0] // batch
        up_rows = jnp.concatenate([up_g, up_v], axis=-1)
        ffn_state = up_rows.reshape(batch, tiles, SUBLANES, -1)[:, -1, SUBLANES - (CONV_W - 1):, :]
        d_ff = hb.shape[1]
        nk = 2 if (d_ff // 2) % LANES == 0 else 1
        x2, x2b = _proj_ln([hb], w_down, [0], x1, ln2_g, ln2_b, alpha=alpha, nk=nk,
                           tm=_pick(m, (512, 256, 128)), tn=1024, tn_out=1024, name="down_ln")
        made = None
    return x2, x2b, (kh, vh, logf_out, conv_state, sgu_v, ffn_state), made


def kernel(x_prompt, x_sample, cache_k, cache_v, cache_logf, page_table, state_conv, state_ffn_conv,
           w_in, b_f, sgu_ln_g, sgu_ln_b, sgu_w, sgu_b, conv_w, conv_b, mix_norm_g, w_o,
           ln1_g, ln1_b, w_up, ffn_conv_w, ffn_conv_b, w_down, ln2_g, ln2_b):
    depth, d, d_in = w_in.shape
    batch, seq_len, _ = x_prompt.shape
    dec_batch, dec_seq, _ = x_sample.shape
    assert dec_seq == 1, "the sample group is one new token per sequence"
    alpha = (2 * depth) ** 0.25
    h_c = b_f.shape[1]
    n_main = d_in - h_c

    w_in_nk = jnp.swapaxes(w_in, 1, 2)
    w_f = jnp.pad(w_in[:, :, n_main:], ((0, 0), (0, 0), (0, HEAD_DIM - h_c))).astype(BF16)
    b_f_p = jnp.pad(b_f, ((0, 0), (0, HEAD_DIM - h_c)))[:, None, :]
    sgu_b_t = jnp.transpose(sgu_b, (0, 2, 1))
    n_pool, page = cache_k.shape[1], cache_k.shape[2]
    cache_k2 = cache_k.reshape(depth, n_pool, page * h_c, HEAD_DIM)
    cache_v2 = cache_v.reshape(depth, n_pool, page * h_c, HEAD_DIM)
    cache_lf = cache_logf.reshape(depth, n_pool, 1, page * h_c)
    row = lambda a: a[:, None, :]

    xp = x_prompt.reshape(batch * seq_len, d)
    xs = x_sample.reshape(dec_batch, d)
    xpb, xsb = xp.astype(BF16), xs.astype(BF16)
    sp, ss = [], []
    for l in range(depth):
        lw = (w_f[l], b_f_p[l], row(sgu_ln_g)[l], row(sgu_ln_b)[l], sgu_w[l], sgu_b_t[l],
              conv_w[l], row(conv_b)[l], row(mix_norm_g)[l], row(ln1_g)[l], row(ln1_b)[l],
              ffn_conv_w[l], row(ffn_conv_b)[l], row(ln2_g)[l], row(ln2_b)[l])
        ctx = dict(conv=state_conv[l], ffn=state_ffn_conv[l], cache_k=cache_k2, cache_v=cache_v2,
                   cache_lf=cache_lf, page_table=page_table, layer=l)
        xs, xsb, st, mats = _layer(xs, xsb, lw, (w_in_nk, w_o, w_up, w_down), alpha=alpha, batch=dec_batch,
                                   seq_len=1, decode_ctx=ctx)
        ss.append(st)
        xp, xpb, st, _ = _layer(xp, xpb, lw, mats, alpha=alpha, batch=batch, seq_len=seq_len)
        sp.append(st)

    def stack(sts, i):
        return jnp.stack([s[i] for s in sts])

    return (xp.reshape(batch, seq_len, d), xs.reshape(dec_batch, 1, d),
            stack(sp, 0), stack(sp, 1), stack(sp, 2), stack(sp, 3), stack(sp, 5), stack(sp, 4),
            stack(ss, 0), stack(ss, 1), stack(ss, 2), stack(ss, 3), stack(ss, 5), stack(ss, 4))
```

```python
import functools

import jax
import jax.numpy as jnp
from jax import lax
from jax.experimental import pallas as pl
from jax.experimental.pallas import tpu as pltpu

HEAD_DIM = 128
CHUNK = 128
CONV_W = 3
EPS = 1e-5
MASK_VALUE = -1e30
LOG2_E = 1.4426950408889634
SUBLANES = 8
LANES = 128
STAT_ROWS = 128
VMEM_LIMIT_BYTES = 56 * 1024 * 1024
PAGES_PER_STEP = 8

F32 = jnp.float32
BF16 = jnp.bfloat16


def _pick(dim, candidates):
    for c in candidates:
        if dim % c == 0:
            return c
    return dim


def _params(*semantics):
    return pltpu.CompilerParams(dimension_semantics=semantics, vmem_limit_bytes=VMEM_LIMIT_BYTES)


CONTRACT_LAST = (((1,), (1,)), ((), ()))


def _mm_nk_kernel(x_ref, w_ref, o_ref):
    o_ref[...] = lax.dot_general(x_ref[...], w_ref[...], CONTRACT_LAST,
                                 preferred_element_type=F32).astype(o_ref.dtype)


def _matmul_nk(x, w, *, col_off=0, n_cols=None, out_dtype=F32, name="mm"):
    m, kdim = x.shape
    n_cols = w.shape[0] if n_cols is None else n_cols
    tm = _pick(m, (1024, 512, 256, 128))
    tn = next(c for c in (1024, 512, 256, 128) if n_cols % c == 0 and col_off % c == 0)
    joff = col_off // tn
    return pl.pallas_call(
        _mm_nk_kernel,
        grid=(m // tm, n_cols // tn),
        in_specs=[pl.BlockSpec((tm, kdim), lambda i, j: (i, 0)),
                  pl.BlockSpec((tn, kdim), lambda i, j: (j + joff, 0))],
        out_specs=pl.BlockSpec((tm, tn), lambda i, j: (i, j)),
        out_shape=jax.ShapeDtypeStruct((m, n_cols), out_dtype),
        compiler_params=_params("parallel", "parallel"),
        name=name,
    )(x, w)


def _cast_mm_kernel(x_ref, w_ref, z_ref, wb_ref, *, nk, w_is_nk):
    wb = w_ref[...].astype(BF16)
    wb_ref[...] = wb
    if w_is_nk:
        part = lax.dot_general(x_ref[...], wb, CONTRACT_LAST, preferred_element_type=F32)
    else:
        part = jnp.dot(x_ref[...], wb, preferred_element_type=F32)
    if nk == 1:
        z_ref[...] = part
        return
    k = pl.program_id(1)

    @pl.when(k == 0)
    def _():
        z_ref[...] = part

    @pl.when(k > 0)
    def _():
        z_ref[...] += part


def _cast_matmul(x, w_all, layer, *, n_cols=None, w_is_nk=False, name="cast_mm"):
    m, kdim = x.shape
    n_cols = w_all.shape[1 if w_is_nk else 2] if n_cols is None else n_cols
    tk = kdim if kdim <= 4096 else _pick(kdim, (5504, 4096, 2048, 1024, 512, 256, 128))
    tn = _pick(n_cols, (512, 256, 128))
    nk = kdim // tk
    if w_is_nk:
        w_spec = pl.BlockSpec((None, tn, tk), lambda j, k: (layer, j, k))
        wb_spec = pl.BlockSpec((tn, tk), lambda j, k: (j, k))
        wb_shape = (n_cols, kdim)
    else:
        w_spec = pl.BlockSpec((None, tk, tn), lambda j, k: (layer, k, j))
        wb_spec = pl.BlockSpec((tk, tn), lambda j, k: (k, j))
        wb_shape = (kdim, n_cols)
    return pl.pallas_call(
        functools.partial(_cast_mm_kernel, nk=nk, w_is_nk=w_is_nk),
        grid=(n_cols // tn, nk),
        in_specs=[pl.BlockSpec((m, tk), lambda j, k: (0, k)), w_spec],
        out_specs=[pl.BlockSpec((m, tn), lambda j, k: (0, j)), wb_spec],
        out_shape=[jax.ShapeDtypeStruct((m, n_cols), F32), jax.ShapeDtypeStruct(wb_shape, BF16)],
        compiler_params=_params("parallel", "arbitrary"),
        name=name,
    )(x, w_all)


def _logf_kernel(x_ref, w_ref, b_ref, o_ref):
    z = jnp.dot(x_ref[...], w_ref[...], preferred_element_type=F32) + b_ref[...]
    o_ref[...] = jnp.minimum(z, 0.0) - jnp.log1p(jnp.exp(-jnp.abs(z)))


def _logf(x, w_f, b_f):
    m, kdim = x.shape
    tm = _pick(m, (512, 256, 128))
    return pl.pallas_call(
        _logf_kernel,
        grid=(m // tm,),
        in_specs=[pl.BlockSpec((tm, kdim), lambda i: (i, 0)),
                  pl.BlockSpec((kdim, HEAD_DIM), lambda i: (0, 0)),
                  pl.BlockSpec((1, HEAD_DIM), lambda i: (0, 0))],
        out_specs=pl.BlockSpec((tm, HEAD_DIM), lambda i: (i, 0)),
        out_shape=jax.ShapeDtypeStruct((m, HEAD_DIM), F32),
        compiler_params=_params("parallel"),
        name="logf",
    )(x, w_f, b_f)


def _head_rms(y, g):
    return y * lax.rsqrt(jnp.mean(y * y, axis=-1, keepdims=True) + EPS) * g


def _shift_rows(c, prev):
    row = lax.broadcasted_iota(jnp.int32, c.shape, 0)
    p1 = prev[SUBLANES - 1:SUBLANES, :]
    p2 = prev[SUBLANES - 2:SUBLANES - 1, :]
    c1 = jnp.where(row == 0, p1, pltpu.roll(c, 1, axis=0))
    c2 = jnp.where(row == 0, p2, jnp.where(row == 1, p1, pltpu.roll(c, 2, axis=0)))
    return c1, c2


def _sgu_kernel(u_ref, v_ref, lng_ref, lnb_ref, w_ref, bst_ref, g_ref, y_ref, vln_ref, *, n_chunks, n_heads):
    u = jax.nn.gelu(u_ref[...].astype(F32))
    v = jax.nn.gelu(v_ref[...].astype(F32))
    mu = jnp.mean(v, axis=-1, keepdims=True)
    d = v - mu
    var = jnp.mean(d * d, axis=-1, keepdims=True)
    vln = d * lax.rsqrt(var + EPS) * lng_ref[...] + lnb_ref[...]
    vln_ref[...] = vln
    vb = vln.astype(BF16)
    t_idx = lax.broadcasted_iota(jnp.int32, (CHUNK, CHUNK), 0)
    s_idx = lax.broadcasted_iota(jnp.int32, (CHUNK, CHUNK), 1)
    causal = s_idx <= t_idx
    for h in range(n_heads):
        cols = slice(h * HEAD_DIM, (h + 1) * HEAD_DIM)
        wm = jnp.where(causal, w_ref[h], 0.0).astype(BF16)
        bias = bst_ref[:, h:h + 1]
        g = g_ref[:, cols]
        for c in range(n_chunks):
            rows = slice(c * CHUNK, (c + 1) * CHUNK)
            mixed = jnp.dot(wm, vb[rows, cols], preferred_element_type=F32) + bias
            y_ref[rows, cols] = _head_rms(u[rows, cols] * mixed, g).astype(y_ref.dtype)


def _sgu(zb, ln_g, ln_b, w_s, b_s_t, norm_g, d_a):
    m = zb.shape[0]
    n_heads = d_a // HEAD_DIM
    tm = _pick(m, (256, 128))
    return pl.pallas_call(
        functools.partial(_sgu_kernel, n_chunks=tm // CHUNK, n_heads=n_heads),
        grid=(m // tm,),
        in_specs=[pl.BlockSpec((tm, d_a), lambda i: (i, 0)),
                  pl.BlockSpec((tm, d_a), lambda i: (i, 1)),
                  pl.BlockSpec((1, d_a), lambda i: (0, 0)),
                  pl.BlockSpec((1, d_a), lambda i: (0, 0)),
                  pl.BlockSpec((n_heads, CHUNK, CHUNK), lambda i: (0, 0, 0)),
                  pl.BlockSpec((CHUNK, n_heads), lambda i: (0, 0)),
                  pl.BlockSpec((1, d_a), lambda i: (0, 0))],
        out_specs=[pl.BlockSpec((tm, d_a), lambda i: (i, 0)),
                   pl.BlockSpec((tm, d_a), lambda i: (i, 0))],
        out_shape=[jax.ShapeDtypeStruct((m, d_a), BF16), jax.ShapeDtypeStruct((m, d_a), F32)],
        compiler_params=_params("parallel"),
        name="sgu",
    )(zb, zb, ln_g, ln_b, w_s, b_s_t, norm_g)


def _bconv_kernel(*refs, tiles_per_seq, n_heads, decode):
    if decode:
        x_ref, gb_ref, gc_ref, cw_ref, cb_ref, g_ref, p2_ref, p1_ref, y_ref, st_ref = refs
    else:
        x_ref, gb_ref, gc_ref, cw_ref, cb_ref, g_ref, y_ref, st_ref, carry_ref = refs
    c = gc_ref[...].astype(F32) * x_ref[...].astype(F32)
    tm = c.shape[0]
    if decode:
        c1, c2 = p1_ref[...], p2_ref[...]
    else:
        start = pl.program_id(0) % tiles_per_seq == 0
        prev = jnp.where(start, 0.0, carry_ref[...])
        c1, c2 = _shift_rows(c, prev)
        carry_ref[...] = c[tm - SUBLANES:, :]
    st_ref[...] = c[tm - SUBLANES:, :]
    conv = cb_ref[...] + cw_ref[0:1, :] * c2 + cw_ref[1:2, :] * c1 + cw_ref[2:3, :] * c
    y = gb_ref[...].astype(F32) * conv
    for h in range(n_heads):
        cols = slice(h * HEAD_DIM, (h + 1) * HEAD_DIM)
        y_ref[:, cols] = _head_rms(y[:, cols], g_ref[:, cols]).astype(y_ref.dtype)


def _bconv(zb, conv_w, conv_b, norm_g, d_b, col_blk, seq_len, state=None):
    m = zb.shape[0]
    decode = state is not None
    tm = m if decode else _pick(seq_len, (512, 256, 128))
    n_tiles = m // tm
    row = lambda i: (i, 0)
    fixed = lambda i: (0, 0)
    in_specs = [pl.BlockSpec((tm, d_b), lambda i: (i, col_blk)),
                pl.BlockSpec((tm, d_b), lambda i: (i, col_blk + 1)),
                pl.BlockSpec((tm, d_b), lambda i: (i, col_blk + 2)),
                pl.BlockSpec((CONV_W, d_b), fixed),
                pl.BlockSpec((1, d_b), fixed),
                pl.BlockSpec((1, d_b), fixed)]
    args = [zb, zb, zb, conv_w, conv_b, norm_g]
    scratch = []
    if decode:
        in_specs += [pl.BlockSpec((tm, d_b), row), pl.BlockSpec((tm, d_b), row)]
        args += [state[:, 0, :], state[:, 1, :]]
    else:
        scratch = [pltpu.VMEM((SUBLANES, d_b), F32)]
    return pl.pallas_call(
        functools.partial(_bconv_kernel, tiles_per_seq=max(seq_len // tm, 1), n_heads=d_b // HEAD_DIM,
                          decode=decode),
        grid=(n_tiles,),
        in_specs=in_specs,
        out_specs=[pl.BlockSpec((tm, d_b), row),
                   pl.BlockSpec((None, SUBLANES, d_b), lambda i: (i, 0, 0))],
        out_shape=[jax.ShapeDtypeStruct((m, d_b), BF16),
                   jax.ShapeDtypeStruct((n_tiles, SUBLANES, d_b), F32)],
        scratch_shapes=scratch,
        compiler_params=_params("arbitrary"),
        name="bconv_decode" if decode else "bconv",
    )(*args)


def _cumsum_kernel(x_ref, o_ref):
    x = x_ref[...]
    t = x.shape[-1]
    lane = lax.broadcasted_iota(jnp.int32, x.shape, 1)
    shift = 1
    while shift < t:
        x = x + jnp.where(lane >= shift, pltpu.roll(x, shift, axis=1), 0.0)
        shift *= 2
    o_ref[...] = x


def _cumsum_time(logf_t):
    b, h, t = logf_t.shape
    return pl.pallas_call(
        _cumsum_kernel,
        grid=(b,),
        in_specs=[pl.BlockSpec((None, h, t), lambda i: (i, 0, 0))],
        out_specs=pl.BlockSpec((None, h, t), lambda i: (i, 0, 0)),
        out_shape=jax.ShapeDtypeStruct((b, h, t), F32),
        compiler_params=_params("parallel"),
        name="cumsum_logf",
    )(logf_t)


def _fox_prompt_kernel(q_ref, k_ref, v_ref, fq_ref, fk_ref, g_ref, o_ref, *, tq, nq):
    scale = HEAD_DIM ** -0.5 * LOG2_E
    kb = k_ref[...].astype(BF16)
    vb = v_ref[...].astype(BF16)
    fk_all = fk_ref[...] * LOG2_E
    causal = (lax.broadcasted_iota(jnp.int32, (tq, tq), 1) <= lax.broadcasted_iota(jnp.int32, (tq, tq), 0))
    for qi in range(nq):
        rows = slice(qi * tq, (qi + 1) * tq)
        q = q_ref[rows, :]
        fq = fq_ref[rows, :] * LOG2_E
        sd = lax.dot_general(q, kb[rows, :], CONTRACT_LAST, preferred_element_type=F32)
        sd = jnp.where(causal, sd * scale + fq - fk_all[:, rows], MASK_VALUE)
        m = jnp.max(sd, axis=-1, keepdims=True)
        if qi > 0:
            past = slice(0, qi * tq)
            so = lax.dot_general(q, kb[past, :], CONTRACT_LAST, preferred_element_type=F32)
            so = so * scale + fq - fk_all[:, past]
            m = jnp.maximum(m, jnp.max(so, axis=-1, keepdims=True))
            po = jnp.exp2(so - m)
            l = jnp.sum(po, axis=-1, keepdims=True)
            acc = jnp.dot(po.astype(BF16), vb[past, :], preferred_element_type=F32)
        pd = jnp.exp2(sd - m)
        pv = jnp.dot(pd.astype(BF16), vb[rows, :], preferred_element_type=F32)
        if qi > 0:
            l = l + jnp.sum(pd, axis=-1, keepdims=True)
            acc = acc + pv
        else:
            l = jnp.sum(pd, axis=-1, keepdims=True)
            acc = pv
        o_ref[rows, :] = _head_rms(acc / l, g_ref[...]).astype(o_ref.dtype)


def _fox_prompt(zb, k, v, f_col, f_row, norm_g, *, batch, seq_len, n_heads, q_col_blk):
    m = zb.shape[0]
    tq = _pick(seq_len, (512, 256, 128))
    seq_blk = lambda b, h: (b, h)
    return pl.pallas_call(
        functools.partial(_fox_prompt_kernel, tq=tq, nq=seq_len // tq),
        grid=(batch, n_heads),
        in_specs=[pl.BlockSpec((seq_len, HEAD_DIM), lambda b, h: (b, q_col_blk + h)),
                  pl.BlockSpec((seq_len, HEAD_DIM), seq_blk),
                  pl.BlockSpec((seq_len, HEAD_DIM), seq_blk),
                  pl.BlockSpec((None, None, seq_len, 1), lambda b, h: (b, h, 0, 0)),
                  pl.BlockSpec((None, None, 1, seq_len), lambda b, h: (b, h, 0, 0)),
                  pl.BlockSpec((1, HEAD_DIM), lambda b, h: (0, h))],
        out_specs=pl.BlockSpec((seq_len, HEAD_DIM), seq_blk),
        out_shape=jax.ShapeDtypeStruct((m, n_heads * HEAD_DIM), BF16),
        compiler_params=_params("parallel", "parallel"),
        name="fox_prompt",
    )(zb, k, v, f_col, f_row, norm_g)


def _fox_decode_kernel(pt_ref, q_ref, kn_ref, vn_ref, lfn_ref, g_ref, *rest, n_heads, page):
    del pt_ref
    pps = PAGES_PER_STEP
    k_refs, v_refs, lf_refs = rest[:pps], rest[pps:2 * pps], rest[2 * pps:3 * pps]
    o_ref, m_ref, l_ref, acc_ref, carry_ref = rest[3 * pps:]
    step = pl.program_id(1)
    scale = HEAD_DIM ** -0.5
    n_rows = page * n_heads

    @pl.when(step == 0)
    def _():
        m_ref[...] = jnp.full(m_ref.shape, MASK_VALUE, F32)
        l_ref[...] = jnp.zeros(l_ref.shape, F32)
        acc_ref[...] = jnp.zeros(acc_ref.shape, F32)
        carry_ref[...] = jnp.zeros(carry_ref.shape, F32)

    q = q_ref[...]
    lf = jnp.concatenate([r[...] for r in lf_refs], axis=0)
    lane = lax.broadcasted_iota(jnp.int32, lf.shape, 1)
    suffix, total = lf, lf
    shift = n_heads
    while shift < n_rows:
        suffix = suffix + jnp.where(lane < n_rows - shift, pltpu.roll(suffix, n_rows - shift, axis=1), 0.0)
        total = total + pltpu.roll(total, shift, axis=1)
        shift *= 2
    carry = carry_ref[...]
    lf_new = lfn_ref[...]
    bias = [None] * pps
    for r in reversed(range(pps)):
        bias[r] = suffix[r:r + 1, :] - lf[r:r + 1, :] + carry + lf_new
        carry = carry + total[r:r + 1, :]
    carry_ref[...] = carry

    own_head = ((lax.broadcasted_iota(jnp.int32, (n_heads, n_rows), 1) & (n_heads - 1))
                == lax.broadcasted_iota(jnp.int32, (n_heads, n_rows), 0))
    contract_last = (((1,), (1,)), ((), ()))
    s = [jnp.where(own_head,
                   lax.dot_general(q, k_refs[r][...].astype(BF16), contract_last,
                                   preferred_element_type=F32) * scale + bias[r],
                   MASK_VALUE) for r in range(pps)]
    m_prev = m_ref[...]
    m_new = m_prev
    for r in range(pps):
        m_new = jnp.maximum(m_new, jnp.max(s[r], axis=-1, keepdims=True))
    alpha = jnp.exp(m_prev - m_new)
    l_new = alpha * l_ref[...]
    acc = alpha * acc_ref[...]
    for r in range(pps):
        p = jnp.exp(s[r] - m_new)
        l_new = l_new + jnp.sum(p, axis=-1, keepdims=True)
        acc = acc + jnp.dot(p.astype(BF16), v_refs[r][...].astype(BF16), preferred_element_type=F32)
    l_ref[...] = l_new
    acc_ref[...] = acc
    m_ref[...] = m_new

    @pl.when(step == pl.num_programs(1) - 1)
    def _():
        kn = kn_ref[...].astype(BF16).astype(F32)
        vn = vn_ref[...].astype(BF16).astype(F32)
        s_self = jnp.sum(q.astype(F32) * kn, axis=-1, keepdims=True) * scale
        m_prev = m_ref[...]
        m_fin = jnp.maximum(m_prev, s_self)
        a = jnp.exp(m_prev - m_fin)
        p_self = jnp.exp(s_self - m_fin)
        l_fin = a * l_ref[...] + p_self
        o = (a * acc_ref[...] + p_self.astype(BF16).astype(F32) * vn) / l_fin
        o_ref[...] = _head_rms(o, g_ref[...]).astype(o_ref.dtype)


def _fox_decode(q, k_new, v_new, lf_new, norm_g, cache_k, cache_v, cache_lf, page_table, layer):
    bsz, n_heads, _ = q.shape
    n_rows = cache_k.shape[2]
    page = n_rows // n_heads
    n_pages = page_table.shape[1]
    pps = PAGES_PER_STEP
    assert n_pages % pps == 0 and n_heads & (n_heads - 1) == 0
    n_steps = n_pages // pps

    def page_idx(b, s, pt, r):
        return pt[b, (n_steps - 1 - s) * pps + r]

    per_b = lambda b, s, pt: (b, 0, 0)
    kv_spec = lambda r: pl.BlockSpec((None, None, n_rows, HEAD_DIM),
                                     lambda b, s, pt: (layer, page_idx(b, s, pt, r), 0, 0))
    lf_spec = lambda r: pl.BlockSpec((None, None, 1, n_rows),
                                     lambda b, s, pt: (layer, page_idx(b, s, pt, r), 0, 0))
    in_specs = ([pl.BlockSpec((None, n_heads, HEAD_DIM), per_b)] * 3
                + [pl.BlockSpec((None, 1, n_rows), per_b),
                   pl.BlockSpec((n_heads, HEAD_DIM), lambda b, s, pt: (0, 0))]
                + [kv_spec(r) for r in range(pps)] * 2
                + [lf_spec(r) for r in range(pps)])
    grid_spec = pltpu.PrefetchScalarGridSpec(
        num_scalar_prefetch=1,
        grid=(bsz, n_steps),
        in_specs=in_specs,
        out_specs=pl.BlockSpec((None, n_heads, HEAD_DIM), per_b),
        scratch_shapes=[pltpu.VMEM((n_heads, 1), F32), pltpu.VMEM((n_heads, 1), F32),
                        pltpu.VMEM((n_heads, HEAD_DIM), F32), pltpu.VMEM((1, n_rows), F32)])
    return pl.pallas_call(
        functools.partial(_fox_decode_kernel, n_heads=n_heads, page=page),
        grid_spec=grid_spec,
        out_shape=jax.ShapeDtypeStruct((bsz, n_heads, HEAD_DIM), BF16),
        compiler_params=_params("parallel", "arbitrary"),
        name="fox_decode",
    )(page_table, q, k_new, v_new, lf_new, norm_g,
      *([cache_k] * pps), *([cache_v] * pps), *([cache_lf] * pps))


def _proj_ln_kernel(*refs, n_lhs, nk, nj, alpha, tn, tn_out):
    lhs_refs, w_refs = refs[:n_lhs], refs[n_lhs:2 * n_lhs]
    res_ref, g_ref, b_ref, o_ref, ob_ref, acc_ref, mu_ref, rs_ref = refs[2 * n_lhs:]
    t = pl.program_id(1)
    n_mm = nk * nj
    cols = pl.ds(pl.multiple_of((t % nj) * tn, tn), tn)

    def product():
        out = jnp.dot(lhs_refs[0][...], w_refs[0][...], preferred_element_type=F32)
        for x_ref, w_ref in zip(lhs_refs[1:], w_refs[1:]):
            out += jnp.dot(x_ref[...], w_ref[...], preferred_element_type=F32)
        return out

    @pl.when(t < nj)
    def _():
        acc_ref[:, cols] = product() + alpha * res_ref[...]

    if nk > 1:
        @pl.when((t >= nj) & (t < n_mm))
        def _():
            acc_ref[:, cols] += product()

    @pl.when(t == n_mm)
    def _():
        def row_stats(r, carry):
            rows = pl.ds(pl.multiple_of(r * STAT_ROWS, STAT_ROWS), STAT_ROWS)
            y = acc_ref[rows, :]
            mu = jnp.mean(y, axis=-1, keepdims=True)
            d = y - mu
            mu_ref[rows, :] = mu
            rs_ref[rows, :] = lax.rsqrt(jnp.mean(d * d, axis=-1, keepdims=True) + EPS)
            return carry
        lax.fori_loop(0, acc_ref.shape[0] // STAT_ROWS, row_stats, 0)

    @pl.when(t >= n_mm)
    def _():
        ocols = pl.ds(pl.multiple_of((t - n_mm) * tn_out, tn_out), tn_out)
        o = (acc_ref[:, ocols] - mu_ref[...]) * rs_ref[...] * g_ref[...] + b_ref[...]
        o_ref[...] = o
        ob_ref[...] = o.astype(ob_ref.dtype)


def _proj_ln(lhs_list, w, w_row_blocks, resid, g, b, *, alpha, nk, tm, tn, tn_out, name):
    m, n = resid.shape
    tn = _pick(n, (tn, 512, 256, 128))
    tn_out = _pick(n, (tn_out, 512, 256, 128))
    nj, n_out = n // tn, n // tn_out
    n_mm = nk * nj
    n_lhs = len(lhs_list)
    widths = [x.shape[1] // nk for x in lhs_list]
    phase = lambda t: jnp.minimum(t, n_mm - 1) // nj
    col = lambda t: jnp.minimum(t, n_mm - 1) % nj
    out_col = lambda t: jnp.maximum(t - n_mm, 0)
    in_specs = [pl.BlockSpec((tm, kw), lambda i, t: (i, phase(t))) for kw in widths]
    in_specs += [pl.BlockSpec((kw, tn), lambda i, t, rb=rb: (rb + phase(t), col(t)))
                 for kw, rb in zip(widths, w_row_blocks)]
    in_specs += [pl.BlockSpec((tm, tn), lambda i, t: (i, jnp.minimum(t, nj - 1))),
                 pl.BlockSpec((1, tn_out), lambda i, t: (0, out_col(t))),
                 pl.BlockSpec((1, tn_out), lambda i, t: (0, out_col(t)))]
    out_spec = pl.BlockSpec((tm, tn_out), lambda i, t: (i, out_col(t)))
    return pl.pallas_call(
        functools.partial(_proj_ln_kernel, n_lhs=n_lhs, nk=nk, nj=nj, alpha=alpha, tn=tn, tn_out=tn_out),
        grid=(m // tm, n_mm + n_out),
        in_specs=in_specs,
        out_specs=[out_spec, out_spec],
        out_shape=[jax.ShapeDtypeStruct((m, n), F32), jax.ShapeDtypeStruct((m, n), BF16)],
        scratch_shapes=[pltpu.VMEM((tm, n), F32), pltpu.VMEM((tm, 1), F32), pltpu.VMEM((tm, 1), F32)],
        compiler_params=_params("parallel", "arbitrary"),
        name=name,
    )(*lhs_list, *([w] * n_lhs), resid, g, b)


def _ln_res_kernel(x_ref, h_ref, g_ref, b_ref, o_ref, ob_ref, *, alpha):
    y = alpha * x_ref[...] + h_ref[...]
    mu = jnp.mean(y, axis=-1, keepdims=True)
    d = y - mu
    var = jnp.mean(d * d, axis=-1, keepdims=True)
    o = d * lax.rsqrt(var + EPS) * g_ref[...] + b_ref[...]
    o_ref[...] = o
    ob_ref[...] = o.astype(ob_ref.dtype)


def _ln_res(x, h, g, b, alpha):
    m, d = x.shape
    tm = _pick(m, (256, 128))
    row = lambda i: (i, 0)
    fixed = lambda i: (0, 0)
    return pl.pallas_call(
        functools.partial(_ln_res_kernel, alpha=alpha),
        grid=(m // tm,),
        in_specs=[pl.BlockSpec((tm, d), row), pl.BlockSpec((tm, d), row),
                  pl.BlockSpec((1, d), fixed), pl.BlockSpec((1, d), fixed)],
        out_specs=[pl.BlockSpec((tm, d), row), pl.BlockSpec((tm, d), row)],
        out_shape=[jax.ShapeDtypeStruct((m, d), F32), jax.ShapeDtypeStruct((m, d), BF16)],
        compiler_params=_params("parallel"),
        name="ln_res",
    )(x, h, g, b)


def _conv_taps(c2, c1, c, cw_ref, cb_ref):
    return cb_ref[...] + cw_ref[0:1, :] * c2 + cw_ref[1:2, :] * c1 + cw_ref[2:3, :] * c


def _ffn_up_kernel(x_ref, wg_ref, wv_ref, cwg_ref, cwv_ref, cbg_ref, cbv_ref, h_ref, sg_ref, sv_ref,
                   raw_g_ref, raw_v_ref, carry_g_ref, carry_v_ref, *, tiles_per_seq, nf, n_tiles):
    s = pl.program_id(0)
    tm = x_ref.shape[0]

    @pl.when(s == 0)
    def _():
        raw_g_ref[...] = jnp.zeros(raw_g_ref.shape, F32)
        raw_v_ref[...] = jnp.zeros(raw_v_ref.shape, F32)

    def taps(raw_ref, cw_ref, cb_ref):
        tf = raw_ref.shape[1]
        nb = tm // SUBLANES
        r3 = raw_ref[...].reshape(nb + 1, SUBLANES, tf)
        rot1 = pltpu.roll(r3, 1, axis=1)
        rot2 = pltpu.roll(r3, 2, axis=1)
        sub = lax.broadcasted_iota(jnp.int32, (nb, SUBLANES, tf), 1)
        c1 = jnp.where(sub < 1, rot1[:-1], rot1[1:])
        c2 = jnp.where(sub < 2, rot2[:-1], rot2[1:])
        out = (cb_ref[...][None] + cw_ref[0:1, :][None] * c2 + cw_ref[1:2, :][None] * c1
               + cw_ref[2:3, :][None] * r3[1:])
        return out.reshape(tm, tf)

    gate = taps(raw_g_ref, cwg_ref, cbg_ref)
    val = taps(raw_v_ref, cwv_ref, cbv_ref)
    h_ref[...] = (gate * jax.nn.sigmoid(gate) * val).astype(h_ref.dtype)
    sg_ref[...] = raw_g_ref[pl.ds(tm, SUBLANES), :]
    sv_ref[...] = raw_v_ref[pl.ds(tm, SUBLANES), :]

    sc = jnp.minimum(s, n_tiles * nf - 1)
    j = sc % nf
    start = (sc // nf) % tiles_per_seq == 0
    x = x_ref[...]
    for w_ref, raw_ref, carry_ref in ((wg_ref, raw_g_ref, carry_g_ref), (wv_ref, raw_v_ref, carry_v_ref)):
        u = jnp.dot(x, w_ref[...], preferred_element_type=F32)
        raw_ref[pl.ds(0, SUBLANES), :] = jnp.where(start, 0.0, carry_ref[j])
        raw_ref[pl.ds(SUBLANES, tm), :] = u
        carry_ref[j] = u[tm - SUBLANES:, :]


def _ffn_up(xb, w_gate, w_val, conv_w, conv_b, seq_len):
    m, d = xb.shape
    d_ff = w_gate.shape[1]
    tf = _pick(d_ff, (256, 128))
    nf = d_ff // tf
    tm = _pick(seq_len, (1024, 512, 256, 128))
    n_tiles = m // tm
    last = n_tiles * nf - 1
    cur = lambda s: jnp.minimum(s, last)
    prev = lambda s: jnp.maximum(s - 1, 0)
    return pl.pallas_call(
        functools.partial(_ffn_up_kernel, tiles_per_seq=seq_len // tm, nf=nf, n_tiles=n_tiles),
        grid=(n_tiles * nf + 1,),
        in_specs=[pl.BlockSpec((tm, d), lambda s: (cur(s) // nf, 0)),
                  pl.BlockSpec((d, tf), lambda s: (0, cur(s) % nf)),
                  pl.BlockSpec((d, tf), lambda s: (0, cur(s) % nf)),
                  pl.BlockSpec((CONV_W, tf), lambda s: (0, prev(s) % nf)),
                  pl.BlockSpec((CONV_W, tf), lambda s: (0, prev(s) % nf + nf)),
                  pl.BlockSpec((1, tf), lambda s: (0, prev(s) % nf)),
                  pl.BlockSpec((1, tf), lambda s: (0, prev(s) % nf + nf))],
        out_specs=[pl.BlockSpec((tm, tf), lambda s: (prev(s) // nf, prev(s) % nf)),
                   pl.BlockSpec((None, SUBLANES, tf), lambda s: (prev(s) // nf, 0, prev(s) % nf)),
                   pl.BlockSpec((None, SUBLANES, tf), lambda s: (prev(s) // nf, 0, prev(s) % nf))],
        out_shape=[jax.ShapeDtypeStruct((m, d_ff), BF16),
                   jax.ShapeDtypeStruct((n_tiles, SUBLANES, d_ff), F32),
                   jax.ShapeDtypeStruct((n_tiles, SUBLANES, d_ff), F32)],
        scratch_shapes=[pltpu.VMEM((tm + SUBLANES, tf), F32), pltpu.VMEM((tm + SUBLANES, tf), F32),
                        pltpu.VMEM((nf, SUBLANES, tf), F32), pltpu.VMEM((nf, SUBLANES, tf), F32)],
        compiler_params=_params("arbitrary"),
        name="ffn_up",
    )(xb, w_gate, w_val, conv_w, conv_w, conv_b, conv_b)


def _ffn_up_decode_kernel(x_ref, wg_ref, wv_ref, cwg_ref, cwv_ref, cbg_ref, cbv_ref,
                          pg2_ref, pg1_ref, pv2_ref, pv1_ref, h_ref, ug_ref, uv_ref, wgb_ref, wvb_ref):
    x = x_ref[...]
    wg = wg_ref[...].astype(BF16)
    wv = wv_ref[...].astype(BF16)
    wgb_ref[...] = wg
    wvb_ref[...] = wv
    ug = jnp.dot(x, wg, preferred_element_type=F32)
    uv = jnp.dot(x, wv, preferred_element_type=F32)
    ug_ref[...] = ug
    uv_ref[...] = uv
    gate = _conv_taps(pg2_ref[...], pg1_ref[...], ug, cwg_ref, cbg_ref)
    val = _conv_taps(pv2_ref[...], pv1_ref[...], uv, cwv_ref, cbv_ref)
    h_ref[...] = (gate * jax.nn.sigmoid(gate) * val).astype(h_ref.dtype)


def _ffn_up_decode(xb, w_up_all, layer, conv_w, conv_b, state):
    m, d = xb.shape
    d_ff = w_up_all.shape[2] // 2
    tf = _pick(d_ff, (256, 128))
    nf = d_ff // tf
    gcol = lambda j: (0, j)
    vcol = lambda j: (0, j + nf)
    s0, s1 = state[:, 0, :], state[:, 1, :]
    return pl.pallas_call(
        _ffn_up_decode_kernel,
        grid=(nf,),
        in_specs=[pl.BlockSpec((m, d), lambda j: (0, 0)),
                  pl.BlockSpec((None, d, tf), lambda j: (layer, 0, j)),
                  pl.BlockSpec((None, d, tf), lambda j: (layer, 0, j + nf)),
                  pl.BlockSpec((CONV_W, tf), gcol), pl.BlockSpec((CONV_W, tf), vcol),
                  pl.BlockSpec((1, tf), gcol), pl.BlockSpec((1, tf), vcol),
                  pl.BlockSpec((m, tf), gcol), pl.BlockSpec((m, tf), gcol),
                  pl.BlockSpec((m, tf), vcol), pl.BlockSpec((m, tf), vcol)],
        out_specs=[pl.BlockSpec((m, tf), gcol), pl.BlockSpec((m, tf), gcol), pl.BlockSpec((m, tf), gcol),
                   pl.BlockSpec((d, tf), gcol), pl.BlockSpec((d, tf), gcol)],
        out_shape=[jax.ShapeDtypeStruct((m, d_ff), BF16),
                   jax.ShapeDtypeStruct((m, d_ff), F32), jax.ShapeDtypeStruct((m, d_ff), F32),
                   jax.ShapeDtypeStruct((d, d_ff), BF16), jax.ShapeDtypeStruct((d, d_ff), BF16)],
        compiler_params=_params("parallel"),
        name="ffn_up_decode",
    )(xb, w_up_all, w_up_all, conv_w, conv_w, conv_b, conv_b, s0, s1, s0, s1)


def _layer(x, xb, lw, mats, *, alpha, batch, seq_len, decode_ctx=None):
    (w_f, b_f, sgu_ln_g, sgu_ln_b, sgu_w, sgu_b_t, conv_w, conv_b, mix_norm_g, ln1_g, ln1_b,
     ffn_conv_w, ffn_conv_b, ln2_g, ln2_b) = lw
    m, d = x.shape
    n_heads = d // HEAD_DIM
    d_a = (n_heads // 4) * HEAD_DIM
    d_c = d - 2 * d_a
    h_c = d_c // HEAD_DIM
    assert d_c == 2 * d_a
    decode = decode_ctx is not None

    n_zb = 2 * d_a + 3 * d_a + d_c
    if decode:
        layer = decode_ctx["layer"]
        w_in_all, w_o_all, w_up_all, w_down_all = mats
        z, w_in = _cast_matmul(xb, w_in_all, layer, n_cols=n_zb + 2 * d_c, w_is_nk=True, name="in_proj_decode")
        zb, k, v = z[:, :n_zb].astype(BF16), z[:, n_zb:n_zb + d_c], z[:, n_zb + d_c:]
    else:
        w_in, w_o, w_gate, w_val, w_down = mats
        zb = _matmul_nk(xb, w_in, col_off=0, n_cols=n_zb, out_dtype=BF16, name="in_proj_mix")
        k = _matmul_nk(xb, w_in, col_off=n_zb, n_cols=d_c, out_dtype=F32, name="in_proj_k")
        v = _matmul_nk(xb, w_in, col_off=n_zb + d_c, n_cols=d_c, out_dtype=F32, name="in_proj_v")
    logf = _logf(xb, w_f, b_f)[:, :h_c]

    g_a, g_b, g_c = mix_norm_g[:, :d_a], mix_norm_g[:, d_a:2 * d_a], mix_norm_g[:, 2 * d_a:]

    if decode:
        zuv = jnp.zeros((m, CHUNK, 2 * d_a), BF16).at[:, 0, :].set(zb[:, :2 * d_a]).reshape(m * CHUNK, 2 * d_a)
        ya_full, vln_full = _sgu(zuv, sgu_ln_g, sgu_ln_b, sgu_w, sgu_b_t, g_a, d_a)
        ya = ya_full.reshape(m, CHUNK, d_a)[:, 0, :]
        sgu_v = vln_full.reshape(m, CHUNK, d_a)[:, :1, :]
        yb, c_rows = _bconv(zb, conv_w, conv_b, g_b, d_a, 2, 1, state=decode_ctx["conv"])
        conv_state = jnp.stack([decode_ctx["conv"][:, 1, :], c_rows[0]], axis=1)
        q3 = zb[:, 5 * d_a:].reshape(m, h_c, HEAD_DIM)
        page = decode_ctx["cache_k"].shape[2] // h_c
        lf_rows = jnp.tile(logf, (1, page))[:, None, :]
        yc = _fox_decode(q3, k.reshape(m, h_c, HEAD_DIM), v.reshape(m, h_c, HEAD_DIM), lf_rows,
                         g_c.reshape(h_c, HEAD_DIM), decode_ctx["cache_k"], decode_ctx["cache_v"],
                         decode_ctx["cache_lf"], decode_ctx["page_table"], decode_ctx["layer"]).reshape(m, d_c)
        kh, vh = k.reshape(m, 1, h_c, HEAD_DIM), v.reshape(m, 1, h_c, HEAD_DIM)
        logf_out = logf.reshape(m, 1, h_c)
    else:
        ya, vln = _sgu(zb, sgu_ln_g, sgu_ln_b, sgu_w, sgu_b_t, g_a, d_a)
        last_chunk = ((seq_len - 1) // CHUNK) * CHUNK
        sgu_v = vln.reshape(batch, seq_len, d_a)[:, last_chunk:, :]
        yb, c_rows = _bconv(zb, conv_w, conv_b, g_b, d_a, 2, seq_len)
        tiles = c_rows.shape[0] // batch
        conv_state = c_rows.reshape(batch, tiles, SUBLANES, d_a)[:, -1, SUBLANES - (CONV_W - 1):, :]
        logf_b = logf.reshape(batch, seq_len, h_c)
        f_cum = _cumsum_time(jnp.transpose(logf_b, (0, 2, 1)))
        yc = _fox_prompt(zb, k, v, f_cum[:, :, :, None], f_cum[:, :, None, :], g_c,
                         batch=batch, seq_len=seq_len, n_heads=h_c, q_col_blk=5 * d_a // HEAD_DIM)
        kh, vh = k.reshape(batch, seq_len, h_c, HEAD_DIM), v.reshape(batch, seq_len, h_c, HEAD_DIM)
        logf_out = logf_b

    if decode:
        h, w_o = _cast_matmul(jnp.concatenate([ya, yb, yc], axis=-1), w_o_all, layer, name="oproj_decode")
        x1, x1b = _ln_res(x, h, ln1_g, ln1_b, alpha)
        hb, up_g, up_v, w_gate, w_val = _ffn_up_decode(x1b, w_up_all, layer, ffn_conv_w, ffn_conv_b,
                                                       decode_ctx["ffn"])
        up_rows = jnp.concatenate([up_g, up_v], axis=-1)
        ffn_state = jnp.stack([decode_ctx["ffn"][:, 1, :], up_rows], axis=1)
        ff, w_down = _cast_matmul(hb, w_down_all, layer, name="down_proj_decode")
        x2, x2b = _ln_res(x1, ff, ln2_g, ln2_b, alpha)
        made = (w_in, w_o, w_gate, w_val, w_down)
    else:
        x1, x1b = _proj_ln([ya, yb, yc], w_o, [0, 1, 1], x, ln1_g, ln1_b, alpha=alpha, nk=1,
                           tm=_pick(m, (1024, 512, 256, 128)), tn=512, tn_out=512, name="oproj_ln")
        hb, up_g, up_v = _ffn_up(x1b, w_gate, w_val, ffn_conv_w, ffn_conv_b, seq_len)
        tiles = up_g.shape[0] // batch
        up_rows = jnp.concatenate([up_g, up_v], axis=-1)
        ffn_state = up_rows.reshape(batch, tiles, SUBLANES, -1)[:, -1, SUBLANES - (CONV_W - 1):, :]
        d_ff = hb.shape[1]
        nk = 2 if (d_ff // 2) % LANES == 0 else 1
        x2, x2b = _proj_ln([hb], w_down, [0], x1, ln2_g, ln2_b, alpha=alpha, nk=nk,
                           tm=_pick(m, (512, 256, 128)), tn=1024, tn_out=1024, name="down_ln")
        made = None
    return x2, x2b, (kh, vh, logf_out, conv_state, sgu_v, ffn_state), made


def kernel(x_prompt, x_sample, cache_k, cache_v, cache_logf, page_table, state_conv, state_ffn_conv,
           w_in, b_f, sgu_ln_g, sgu_ln_b, sgu_w, sgu_b, conv_w, conv_b, mix_norm_g, w_o,
           ln1_g, ln1_b, w_up, ffn_conv_w, ffn_conv_b, w_down, ln2_g, ln2_b):
    depth, d, d_in = w_in.shape
    batch, seq_len, _ = x_prompt.shape
    dec_batch, dec_seq, _ = x_sample.shape
    assert dec_seq == 1, "the sample group is one new token per sequence"
    alpha = (2 * depth) ** 0.25
    h_c = b_f.shape[1]
    n_main = d_in - h_c

    w_in_nk = jnp.swapaxes(w_in, 1, 2)
    w_f = jnp.pad(w_in[:, :, n_main:], ((0, 0), (0, 0), (0, HEAD_DIM - h_c))).astype(BF16)
    b_f_p = jnp.pad(b_f, ((0, 0), (0, HEAD_DIM - h_c)))[:, None, :]
    sgu_b_t = jnp.transpose(sgu_b, (0, 2, 1))
    n_pool, page = cache_k.shape[1], cache_k.shape[2]
    cache_k2 = cache_k.reshape(depth, n_pool, page * h_c, HEAD_DIM)
    cache_v2 = cache_v.reshape(depth, n_pool, page * h_c, HEAD_DIM)
    cache_lf = cache_logf.reshape(depth, n_pool, 1, page * h_c)
    row = lambda a: a[:, None, :]

    xp = x_prompt.reshape(batch * seq_len, d)
    xs = x_sample.reshape(dec_batch, d)
    xpb, xsb = xp.astype(BF16), xs.astype(BF16)
    sp, ss = [], []
    for l in range(depth):
        lw = (w_f[l], b_f_p[l], row(sgu_ln_g)[l], row(sgu_ln_b)[l], sgu_w[l], sgu_b_t[l],
              conv_w[l], row(conv_b)[l], row(mix_norm_g)[l], row(ln1_g)[l], row(ln1_b)[l],
              ffn_conv_w[l], row(ffn_conv_b)[l], row(ln2_g)[l], row(ln2_b)[l])
        ctx = dict(conv=state_conv[l], ffn=state_ffn_conv[l], cache_k=cache_k2, cache_v=cache_v2,
                   cache_lf=cache_lf, page_table=page_table, layer=l)
        xs, xsb, st, mats = _layer(xs, xsb, lw, (w_in_nk, w_o, w_up, w_down), alpha=alpha, batch=dec_batch,
                                   seq_len=1, decode_ctx=ctx)
        ss.append(st)
        xp, xpb, st, _ = _layer(xp, xpb, lw, mats, alpha=alpha, batch=batch, seq_len=seq_len)
        sp.append(st)

    def stack(sts, i):
        return jnp.stack([s[i] for s in sts])

    return (xp.reshape(batch, seq_len, d), xs.reshape(dec_batch, 1, d),
            stack(sp, 0), stack(sp, 1), stack(sp, 2), stack(sp, 3), stack(sp, 5), stack(sp, 4),
            stack(ss, 0), stack(ss, 1), stack(ss, 2), stack(ss, 3), stack(ss, 5), stack(ss, 4))
```
